```python
import jax, jax.numpy as jnp
from jax import lax
import numpy as np

D_MODEL = 2048
BATCH = 16
SEQ = 2048
DEPTH = 1

N_META = 16
NORM_EPS = 1e-6
D_FF = ((8 * D_MODEL // 3 + 255) // 256) * 256
RWKV_HEAD = 64
RWKV_HEADS = D_MODEL // RWKV_HEAD
RWKV_WIDTH = RWKV_HEADS * RWKV_HEAD
W_LORA = 96
A_LORA = 96
G_LORA = 256
GN_EPS = RWKV_HEAD * 1e-5
MLA_HEADS = D_MODEL // 128
Q_LORA = 512
KV_LORA = 512
NOPE_DIM = 128
ROPE_DIM = 64
V_DIM = 128
QK_DIM = NOPE_DIM + ROPE_DIM
ROPE_THETA = 10000.0
Q_BLOCK = 128
RWKV_COLS = 3 * RWKV_WIDTH + W_LORA + A_LORA + G_LORA
MLA_COLS = Q_LORA + KV_LORA + ROPE_DIM
GATE_COLS = 2 * D_MODEL
IN_COLS = RWKV_COLS + MLA_COLS + GATE_COLS

kernel_name = "macaron_rwkv7_mla_gated_hybrid"


def _split(x, sizes):
    idx = np.cumsum(sizes)[:-1].tolist()
    return jnp.split(x, idx, axis=-1)


def rms_norm(x, g):
    xf = x.astype(jnp.float32)
    y = xf * lax.rsqrt(jnp.mean(xf * xf, axis=-1, keepdims=True) + NORM_EPS)
    return (y * g.astype(jnp.float32)).astype(x.dtype)


def swiglu(h, w_gate, w_up, w_down):
    return (jax.nn.silu(h @ w_gate) * (h @ w_up)) @ w_down


def rope(x, cos, sin):
    x1, x2 = jnp.split(x.astype(jnp.float32), 2, axis=-1)
    c, s = cos[:, None, :], sin[:, None, :]
    return jnp.concatenate([x1 * c - x2 * s, x1 * s + x2 * c], axis=-1).astype(x.dtype)


def wkv7_scan(r, w, k, v, kk_neg, b):
    bsz, _, h, n = r.shape

    def step(S, inp):
        r_t, w_t, k_t, v_t, kn_t, b_t = inp
        sa = jnp.einsum("bhij,bhj->bhi", S, kn_t)
        S = S * w_t[:, :, None, :] + sa[..., None] * b_t[:, :, None, :] + v_t[..., None] * k_t[:, :, None, :]
        return S, jnp.einsum("bhij,bhj->bhi", S, r_t)

    xs = tuple(jnp.moveaxis(t, 1, 0) for t in (r, w, k, v, kk_neg, b))
    S0 = jnp.zeros((bsz, h, n, n), jnp.float32)
    _, y = lax.scan(step, S0, xs)
    return jnp.moveaxis(y, 0, 1)


def rwkv7_branch(p, mu, w0, w_up, a0, a_up, g_up, k_k, k_a, r_k, gn_w, gn_b):
    bsz, t, _ = p.shape
    prev = jnp.pad(p, ((0, 0), (1, 0), (0, 0)))[:, :-1]
    p = p + mu * (prev - p)
    r, k, v, xw, xa, xg = _split(p, (RWKV_WIDTH, RWKV_WIDTH, RWKV_WIDTH, W_LORA, A_LORA, G_LORA))
    w_pre = -jax.nn.softplus(-(w0 + jnp.tanh(xw) @ w_up)) - 0.5
    decay = jnp.exp(-jnp.exp(w_pre.astype(jnp.float32)))
    a = jax.nn.sigmoid(a0 + xa @ a_up)
    g = jax.nn.sigmoid(xg) @ g_up
    heads = lambda z: z.reshape(bsz, t, RWKV_HEADS, RWKV_HEAD).astype(jnp.float32)
    kk = heads(k * k_k)
    kk = kk * lax.rsqrt(jnp.maximum(jnp.sum(kk * kk, axis=-1, keepdims=True), 1e-24))
    k = k * (1.0 + (a - 1.0) * k_a)
    rh, kh, vh, ah, wh = heads(r), heads(k), heads(v), heads(a), heads(decay)
    y = wkv7_scan(rh, wh, kh, vh, -kk, kk * ah)
    mean = jnp.mean(y, axis=-1, keepdims=True)
    var = jnp.mean(jnp.square(y - mean), axis=-1, keepdims=True)
    y = ((y - mean) * lax.rsqrt(var + GN_EPS)).reshape(bsz, t, RWKV_WIDTH)
    y = y * gn_w.astype(jnp.float32) + gn_b.astype(jnp.float32)
    bonus = jnp.sum(rh * kh * r_k.astype(jnp.float32), axis=-1, keepdims=True) * vh
    y = y + bonus.reshape(bsz, t, RWKV_WIDTH)
    return (y * g.astype(jnp.float32)).astype(p.dtype)


def causal_block_attention(q, k, v):
    bsz, t, h, dq = q.shape
    scale = QK_DIM ** -0.5
    kpos = jnp.arange(t)

    def attend(qb, qpos):
        s = jnp.einsum("bqhd,bkhd->bhqk", qb, k).astype(jnp.float32) * scale
        s = jnp.where(kpos[None, :] <= qpos[:, None], s, -1e30)
        p = jax.nn.softmax(s, axis=-1).astype(v.dtype)
        return jnp.einsum("bhqk,bkhd->bqhd", p, v)

    meta_out = attend(q[:, :N_META], jnp.arange(N_META))
    n_real = t - N_META
    nblk = n_real // Q_BLOCK
    q_real = jnp.moveaxis(q[:, N_META:].reshape(bsz, nblk, Q_BLOCK, h, dq), 1, 0)

    def body(args):
        qb, i = args
        return attend(qb, N_META + i * Q_BLOCK + jnp.arange(Q_BLOCK))

    real = lax.map(body, (q_real, jnp.arange(nblk)))
    real = jnp.moveaxis(real, 0, 1).reshape(bsz, n_real, h, v.shape[-1])
    return jnp.concatenate([meta_out, real], axis=1)


def mla_branch(p, q_norm, w_uq, kv_norm, w_ukv, cos, sin):
    bsz, t, _ = p.shape
    c_q, c_kv, k_pe = _split(p, (Q_LORA, KV_LORA, ROPE_DIM))
    q = (rms_norm(c_q, q_norm) @ w_uq).reshape(bsz, t, MLA_HEADS, QK_DIM)
    q_nope, q_pe = _split(q, (NOPE_DIM, ROPE_DIM))
    kv = (rms_norm(c_kv, kv_norm) @ w_ukv).reshape(bsz, t, MLA_HEADS, NOPE_DIM + V_DIM)
    k_nope, v = _split(kv, (NOPE_DIM, V_DIM))
    q_pe = rope(q_pe, cos, sin)
    k_pe = rope(k_pe[:, :, None, :], cos, sin)
    q = jnp.concatenate([q_nope, q_pe], axis=-1)
    k = jnp.concatenate([k_nope, jnp.broadcast_to(k_pe, (bsz, t, MLA_HEADS, ROPE_DIM))], axis=-1)
    o = causal_block_attention(q, k, v)
    return o.reshape(bsz, t, MLA_HEADS * V_DIM)


def setup_inputs(seed: int = 0) -> dict:
    key = jax.random.key(seed)
    ks = jax.random.split(key, 32)
    L, D = DEPTH, D_MODEL
    nrm = lambda k, shape, s: jax.random.normal(k, shape, jnp.float32) * s
    uni = lambda k, shape: jax.random.uniform(k, shape, jnp.float32)
    return {
        "x": nrm(ks[0], (BATCH, SEQ, D), 1.0),
        "meta_tokens": nrm(ks[1], (N_META, D), 1.0),
        "ffn1_norm": 1.0 + nrm(ks[2], (L, D), 0.02),
        "ffn1_w_gate": nrm(ks[3], (L, D, D_FF), D ** -0.5),
        "ffn1_w_up": nrm(ks[4], (L, D, D_FF), D ** -0.5),
        "ffn1_w_down": nrm(ks[5], (L, D_FF, D), D_FF ** -0.5),
        "mix_norm": 1.0 + nrm(ks[6], (L, D), 0.02),
        "w_in": nrm(ks[7], (L, D, IN_COLS), D ** -0.5),
        "tm_mu": uni(ks[8], (L, RWKV_COLS)),
        "w0": -6.5 + 5.0 * uni(ks[9], (L, RWKV_WIDTH)),
        "w_up": nrm(ks[10], (L, W_LORA, RWKV_WIDTH), W_LORA ** -0.5),
        "a0": nrm(ks[11], (L, RWKV_WIDTH), 0.1),
        "a_up": nrm(ks[12], (L, A_LORA, RWKV_WIDTH), A_LORA ** -0.5),
        "g_up": nrm(ks[13], (L, G_LORA, RWKV_WIDTH), G_LORA ** -0.5),
        "k_k": 0.85 + nrm(ks[14], (L, RWKV_WIDTH), 0.02),
        "k_a": 1.0 + nrm(ks[15], (L, RWKV_WIDTH), 0.02),
        "r_k": nrm(ks[16], (L, RWKV_HEADS, RWKV_HEAD), 0.1),
        "gn_w": 1.0 + nrm(ks[17], (L, RWKV_WIDTH), 0.02),
        "gn_b": nrm(ks[18], (L, RWKV_WIDTH), 0.01),
        "q_norm": 1.0 + nrm(ks[19], (L, Q_LORA), 0.02),
        "w_uq": nrm(ks[20], (L, Q_LORA, MLA_HEADS * QK_DIM), Q_LORA ** -0.5),
        "kv_norm": 1.0 + nrm(ks[21], (L, KV_LORA), 0.02),
        "w_ukv": nrm(ks[22], (L, KV_LORA, MLA_HEADS * (NOPE_DIM + V_DIM)), KV_LORA ** -0.5),
        "w_out": nrm(ks[23], (L, D, D), D ** -0.5),
        "ffn2_norm": 1.0 + nrm(ks[24], (L, D), 0.02),
        "ffn2_w_gate": nrm(ks[25], (L, D, D_FF), D ** -0.5),
        "ffn2_w_up": nrm(ks[26], (L, D, D_FF), D ** -0.5),
        "ffn2_w_down": nrm(ks[27], (L, D_FF, D), D_FF ** -0.5),
        "final_norm": 1.0 + nrm(ks[28], (D,), 0.02),
    }


def reference(x, meta_tokens, ffn1_norm, ffn1_w_gate, ffn1_w_up, ffn1_w_down, mix_norm, w_in,
              tm_mu, w0, w_up, a0, a_up, g_up, k_k, k_a, r_k, gn_w, gn_b, q_norm, w_uq,
              kv_norm, w_ukv, w_out, ffn2_norm, ffn2_w_gate, ffn2_w_up, ffn2_w_down, final_norm):
    bsz = x.shape[0]
    h = jnp.concatenate([jnp.broadcast_to(meta_tokens.astype(x.dtype)[None], (bsz, N_META, D_MODEL)), x], axis=1)
    t = h.shape[1]
    pos = jnp.arange(t, dtype=jnp.float32)
    inv_freq = 1.0 / (ROPE_THETA ** (jnp.arange(0, ROPE_DIM, 2, dtype=jnp.float32) / ROPE_DIM))
    ang = pos[:, None] * inv_freq[None, :]
    cos, sin = jnp.cos(ang), jnp.sin(ang)

    for l in range(DEPTH):
        h = h + 0.5 * swiglu(rms_norm(h, ffn1_norm[l]), ffn1_w_gate[l], ffn1_w_up[l], ffn1_w_down[l])
        u = rms_norm(h, mix_norm[l])
        proj = u @ w_in[l]
        p_rwkv, p_mla, p_gate = _split(proj, (RWKV_COLS, MLA_COLS, GATE_COLS))
        y_a = rwkv7_branch(p_rwkv, tm_mu[l], w0[l], w_up[l], a0[l], a_up[l], g_up[l],
                           k_k[l], k_a[l], r_k[l], gn_w[l], gn_b[l])
        y_b = mla_branch(p_mla, q_norm[l], w_uq[l], kv_norm[l], w_ukv[l], cos, sin)
        g_a, g_b = jnp.split(jax.nn.sigmoid(p_gate), 2, axis=-1)
        h = h + (g_a * y_a + g_b * y_b) @ w_out[l]
        h = h + 0.5 * swiglu(rms_norm(h, ffn2_norm[l]), ffn2_w_gate[l], ffn2_w_up[l], ffn2_w_down[l])

    y = rms_norm(h, final_norm)[:, N_META:]
    return y
```

```python
import functools

import numpy as np
import jax
import jax.numpy as jnp
from jax import lax
from jax.experimental import pallas as pl
from jax.experimental.pallas import tpu as pltpu

F32 = jnp.float32
BF16 = jnp.bfloat16

N_META = 16
NORM_EPS = 1e-6
RWKV_HEAD = 64
GN_EPS = RWKV_HEAD * 1e-5
NOPE_DIM = 128
ROPE_DIM = 64
V_DIM = 128
QK_DIM = NOPE_DIM + ROPE_DIM
ROPE_THETA = 10000.0

LANES = 128
GROUP_HEADS = 4
GROUP_COLS = GROUP_HEADS * RWKV_HEAD
CHUNK = 64
LORA_PAD = 128
QHEAD_COLS = 2 * LANES
META_ROWS = 64
VMEM_LIMIT = 56 * 1024 * 1024


def _cparams(sem):
    return pltpu.CompilerParams(dimension_semantics=sem, vmem_limit_bytes=VMEM_LIMIT)


def _sigmoid(x):
    return 1.0 / (1.0 + jnp.exp(-x))


def _dot(a, b):
    return jnp.dot(a, b, preferred_element_type=F32)


def _dot_nt(a, b):
    return lax.dot_general(a, b, (((1,), (1,)), ((), ())), preferred_element_type=F32)


def _dot_tn(a, b):
    return lax.dot_general(a, b, (((0,), (0,)), ((), ())), preferred_element_type=F32)


def _rms(x, g):
    ms = jnp.mean(x * x, axis=-1, keepdims=True)
    return x * lax.rsqrt(ms + NORM_EPS) * g


def _ffn_kernel(x_ref, g_ref, wg_ref, wu_ref, wd_ref, g2_ref, *rest, emit_h):
    if emit_h:
        h_ref, n_ref, xn_scr, acc_scr = rest
    else:
        n_ref, xn_scr, acc_scr = rest
    j = pl.program_id(1)

    @pl.when(j == 0)
    def _():
        xn_scr[...] = _rms(x_ref[...], g_ref[...]).astype(BF16)
        acc_scr[...] = jnp.zeros_like(acc_scr)

    xn = xn_scr[...]
    a = _dot(xn, wg_ref[...])
    b = _dot(xn, wu_ref[...])
    mid = (a * _sigmoid(a) * b).astype(BF16)
    acc_scr[...] += _dot(mid, wd_ref[...])

    @pl.when(j == pl.num_programs(1) - 1)
    def _():
        h = x_ref[...] + 0.5 * acc_scr[...]
        if emit_h:
            h_ref[...] = h
        n_ref[...] = _rms(h, g2_ref[...]).astype(n_ref.dtype)


def _ffn(x, g, wg, wu, wd, g2, *, emit_h, norm_dtype):
    m, d = x.shape
    dff = wg.shape[1]
    tm = min(512, m)
    tf = 512
    grid = (m // tm, dff // tf)
    row = pl.BlockSpec((tm, d), lambda i, j: (i, 0))
    vec = pl.BlockSpec((1, d), lambda i, j: (0, 0))
    out_shape = [jax.ShapeDtypeStruct((m, d), norm_dtype)]
    out_specs = [row]
    if emit_h:
        out_shape = [jax.ShapeDtypeStruct((m, d), F32)] + out_shape
        out_specs = [row] + out_specs
    return pl.pallas_call(
        functools.partial(_ffn_kernel, emit_h=emit_h),
        grid=grid,
        in_specs=[row, vec,
                  pl.BlockSpec((d, tf), lambda i, j: (0, j)),
                  pl.BlockSpec((d, tf), lambda i, j: (0, j)),
                  pl.BlockSpec((tf, d), lambda i, j: (j, 0)),
                  vec],
        out_specs=out_specs,
        out_shape=out_shape,
        scratch_shapes=[pltpu.VMEM((tm, d), BF16), pltpu.VMEM((tm, d), F32)],
        compiler_params=_cparams(("parallel", "arbitrary")),
        name="ffn",
    )(x, g, wg, wu, wd, g2)


def _matmul_kernel(x_ref, w_ref, o_ref):
    o_ref[...] = _dot(x_ref[...], w_ref[...]).astype(o_ref.dtype)


def _matmul(x, w, out_dtype):
    m, k = x.shape
    n = w.shape[1]
    tm = min(1024, m)
    tn = 1024
    return pl.pallas_call(
        _matmul_kernel,
        grid=(m // tm, n // tn),
        in_specs=[pl.BlockSpec((tm, k), lambda i, j: (i, 0)),
                  pl.BlockSpec((k, tn), lambda i, j: (0, j))],
        out_specs=pl.BlockSpec((tm, tn), lambda i, j: (i, j)),
        out_shape=jax.ShapeDtypeStruct((m, n), out_dtype),
        compiler_params=_cparams(("parallel", "parallel")),
        name="inproj",
    )(x, w)


def _iota(shape, dim):
    return lax.broadcasted_iota(jnp.int32, shape, dim)


def _split3(x):
    hi = x.astype(BF16)
    r1 = x - hi.astype(F32)
    mid = r1.astype(BF16)
    lo = (r1 - mid.astype(F32)).astype(BF16)
    return hi, mid, lo


def _dot01(m01, x):
    hi, mid, lo = _split3(x)
    return _dot(m01, hi) + _dot(m01, mid) + _dot(m01, lo)


def _dot01_r(x, m01):
    hi, mid, lo = _split3(x)
    return _dot(hi, m01) + _dot(mid, m01) + _dot(lo, m01)


class _WkvMasks:
    def __init__(self, L, tt):
        C, H = GROUP_COLS, GROUP_HEADS
        self.L = L
        self.bd_lc = (_iota((H * L, C), 0) // L) == (_iota((H * L, C), 1) // RWKV_HEAD)
        self.bd_ll = (_iota((H * L, H * L), 0) // L) == (_iota((H * L, H * L), 1) // L)
        self.bd_cc = (_iota((C, C), 0) // RWKV_HEAD) == (_iota((C, C), 1) // RWKV_HEAD)
        t = _iota((L, H * L), 0)
        s = _iota((L, H * L), 1) % L
        self.strict = t > s
        self.incl = t >= s
        self.eye = (t == s).astype(F32)
        rt, ct = _iota((tt, tt), 0), _iota((tt, tt), 1)
        self.tri = jnp.logical_and(rt >= ct, rt // L == ct // L).astype(BF16)
        self.seg = self.bd_cc.astype(BF16)

    def bd(self, x):
        return jnp.where(self.bd_lc, jnp.concatenate([x] * GROUP_HEADS, axis=0), 0.0)

    def bd_sq(self, p):
        return jnp.where(self.bd_ll, jnp.concatenate([p] * GROUP_HEADS, axis=0), 0.0)


def _wkv_local(mk, a_t, r_t, b_t, k_t, b_h, k_h, v):
    L = mk.L
    n_h = GROUP_HEADS * L
    C = GROUP_COLS

    def each(f, *lists):
        return [f(*xs) for xs in zip(*lists)]

    def cat(axis, *xs):
        return jnp.concatenate(xs, axis=axis)

    bf = lambda x: x.astype(BF16)
    ar = each(lambda a, r: bf(cat(0, a, r)), a_t, r_t)
    bk = each(lambda b, k: bf(cat(0, mk.bd(b), mk.bd(k))), b_t, k_t)
    aa = each(_dot_nt, ar, bk)
    a_ab = each(lambda x: jnp.where(mk.strict, x[:L, :n_h], 0.0), aa)
    a_ak = each(lambda x: bf(jnp.where(mk.strict, x[:L, n_h:], 0.0)), aa)
    a_rb = each(lambda x: bf(jnp.where(mk.incl, x[L:, :n_h], 0.0)), aa)
    a_rk = each(lambda x: bf(jnp.where(mk.incl, x[L:, n_h:], 0.0)), aa)

    n_steps = int(np.log2(L))
    tp = each(lambda x: mk.eye + x, a_ab)
    p = each(lambda x: _dot(bf(x), bf(mk.bd_sq(x))), a_ab)
    for _ in range(1, n_steps - 1):
        out = each(lambda x, t: _dot(bf(cat(0, x, t)), bf(mk.bd_sq(x))), p, tp)
        p = each(lambda o: o[:L], out)
        tp = each(lambda t, o: t + o[L:], tp, out)
    tb = each(lambda t, x: bf(t + _dot(bf(t), bf(mk.bd_sq(x)))), tp, p)

    v_bd = each(lambda x: bf(mk.bd(x)), v)
    x0 = each(_dot, a_ak, v_bd)
    ut = each(lambda t, x, a: _dot(t, bf(cat(1, mk.bd(x), mk.bd(a)))), tb, x0, a_t)
    u0 = each(lambda x: x[:, :C], ut)
    ta = each(lambda x: x[:, C:], ut)
    q_eff = each(lambda r, ab, x: bf(r + _dot(ab, bf(mk.bd(x)))), r_t, a_rb, ta)
    y0 = each(lambda ab, ak, u, vb: _dot(cat(1, ab, ak), cat(0, bf(mk.bd(u)), vb)),
              a_rb, a_rk, u0, v_bd)
    g = each(lambda x, b: bf(jnp.where(mk.bd_cc, _dot_tn(bf(x), bf(b)), 0.0)), ta, b_h)
    n = each(lambda u, x, b, k: jnp.where(mk.bd_cc, _dot_tn(bf(cat(0, u, x)), bf(cat(0, b, k))), 0.0),
             u0, v, b_h, k_h)
    return q_eff, y0, g, n


def _shift_rows(x, carry_row):
    prev = pltpu.roll(x, 1, axis=0)
    return jnp.where(_iota(x.shape, 0) == 0, carry_row, prev)


def _rwkv_kernel(pr_ref, pk_ref, pv_ref, plo_ref, cr_ref, ck_ref, cv_ref, clo_ref, s0_ref,
                 mur_ref, muk_ref, muv_ref, mulo_ref, w0_ref, a0_ref, kk_ref, ka_ref, rk_ref,
                 gnw_ref, gnb_ref, wup_ref, aup_ref, gup_ref,
                 y_ref, *rest, L, emit_state):
    if emit_state:
        sout_ref, s_scr, crkv_scr, clo_scr = rest
    else:
        s_scr, crkv_scr, clo_scr = rest
    t = pl.program_id(1)
    g = pl.program_id(2)
    n_g = pl.num_programs(2)
    tt = pr_ref.shape[0]
    mk = _WkvMasks(L, tt)

    @pl.when(t == 0)
    def _():
        s_scr[g] = s0_ref[g]
        crkv_scr[3 * g + 0] = cr_ref[...]
        crkv_scr[3 * g + 1] = ck_ref[...]
        crkv_scr[3 * g + 2] = cv_ref[...]

    @pl.when(jnp.logical_and(t == 0, g == 0))
    def _():
        clo_scr[...] = clo_ref[...]

    def shifted(p_ref, carry16, mu):
        x = p_ref[...].astype(F32)
        prev = _shift_rows(x, carry16[15:16, :].astype(F32))
        return x + mu * (prev - x)

    r = shifted(pr_ref, crkv_scr[3 * g + 0], mur_ref[...])
    k = shifted(pk_ref, crkv_scr[3 * g + 1], muk_ref[...])
    v = shifted(pv_ref, crkv_scr[3 * g + 2], muv_ref[...])
    lo = shifted(plo_ref, clo_scr[...], mulo_ref[...])
    crkv_scr[3 * g + 0] = pr_ref[tt - 16:tt, :]
    crkv_scr[3 * g + 1] = pk_ref[tt - 16:tt, :]
    crkv_scr[3 * g + 2] = pv_ref[tt - 16:tt, :]

    @pl.when(g == n_g - 1)
    def _():
        clo_scr[...] = plo_ref[tt - 16:tt, :]

    xw = lo[:, 0:LORA_PAD]
    xa = lo[:, LORA_PAD:2 * LORA_PAD]
    xg = lo[:, 2 * LORA_PAD:]
    z = -(w0_ref[...] + _dot(jnp.tanh(xw).astype(BF16), wup_ref[...]))
    softplus = jnp.maximum(z, 0.0) + jnp.log1p(jnp.exp(-jnp.abs(z)))
    lw = -jnp.exp(-softplus - 0.5)
    alpha = _sigmoid(a0_ref[...] + _dot(xa.astype(BF16), aup_ref[...]))
    gate = _dot(_sigmoid(xg).astype(BF16), gup_ref[...])
    kkr = k * kk_ref[...]
    kk = kkr * lax.rsqrt(jnp.maximum(_dot01_r(kkr * kkr, mk.seg), 1e-24))
    k2 = k * (1.0 + (alpha - 1.0) * ka_ref[...])

    cum = _dot01(mk.tri, lw)
    b = kk * alpha
    e_n = jnp.exp(-cum)
    a_t = -kk * jnp.exp(cum - lw)
    r_t = r * jnp.exp(cum)
    b_t = b * e_n
    k_t = k2 * e_n

    chunks = [slice(c * L, (c + 1) * L) for c in range(tt // L)]
    cl = [cum[sl.stop - 1:sl.stop, :] for sl in chunks]
    e_l = [jnp.exp(c_last - cum[sl]) for c_last, sl in zip(cl, chunks)]
    cut = lambda x: [x[sl] for sl in chunks]
    local = _wkv_local(mk, cut(a_t), cut(r_t), cut(b_t), cut(k_t),
                       [b[sl] * e for sl, e in zip(chunks, e_l)],
                       [k2[sl] * e for sl, e in zip(chunks, e_l)], cut(v))
    S = s_scr[g]
    ys = []
    for w_l, q_eff, y0, g_m, n_m in zip([jnp.exp(c_last) for c_last in cl], *local):
        sb = S.astype(BF16)
        ys.append(y0 + _dot_nt(q_eff, sb))
        S = S * w_l + _dot(sb, g_m) + n_m
    s_scr[g] = S
    if emit_state:
        sout_ref[0] = S
    y = ys[0] if len(ys) == 1 else jnp.concatenate(ys, axis=0)

    inv_n = 1.0 / RWKV_HEAD
    mean = _dot01_r(y, mk.seg) * inv_n
    d = y - mean
    var = _dot01_r(d * d, mk.seg) * inv_n
    yn = d * lax.rsqrt(var + GN_EPS) * gnw_ref[...] + gnb_ref[...]
    bonus = _dot01_r(r * k2 * rk_ref[...], mk.seg) * v
    y_ref[...] = ((yn + bonus) * gate).astype(y_ref.dtype)


def _rwkv(proj, carry16, s0, rp, *, bsz, seq, d, off_rkv, off_lo, emit_state):
    C = GROUP_COLS
    n_g = d // C
    tt = min(256, seq)
    L = min(CHUNK, tt)
    n_t = seq // tt
    assert n_t == 1 or not emit_state
    lo_w = 4 * LORA_PAD
    cb = off_rkv // C
    lb = off_lo // lo_w

    def pblk(j):
        return pl.BlockSpec((tt, C), lambda b, t, g, j=j: (b * n_t + t, cb + j * n_g + g))

    def cblk(j):
        return pl.BlockSpec((16, C), lambda b, t, g, j=j: (0, cb + j * n_g + g))

    def vblk(j=0):
        return pl.BlockSpec((1, C), lambda b, t, g, j=j: (0, j * n_g + g))

    in_specs = [
        pblk(0), pblk(1), pblk(2),
        pl.BlockSpec((tt, lo_w), lambda b, t, g: (b * n_t + t, lb)),
        cblk(0), cblk(1), cblk(2),
        pl.BlockSpec((16, lo_w), lambda b, t, g: (0, lb)),
        pl.BlockSpec((n_g, C, C), lambda b, t, g: (0, 0, 0)),
        vblk(0), vblk(1), vblk(2),
        pl.BlockSpec((1, lo_w), lambda b, t, g: (0, 0)),
        vblk(), vblk(), vblk(), vblk(), vblk(), vblk(), vblk(),
        pl.BlockSpec((LORA_PAD, C), lambda b, t, g: (0, g)),
        pl.BlockSpec((LORA_PAD, C), lambda b, t, g: (0, g)),
        pl.BlockSpec((2 * LORA_PAD, C), lambda b, t, g: (0, g)),
    ]
    out_specs = [pl.BlockSpec((tt, C), lambda b, t, g: (b * n_t + t, g))]
    out_shape = [jax.ShapeDtypeStruct((bsz * seq, d), BF16)]
    if emit_state:
        out_specs.append(pl.BlockSpec((1, C, C), lambda b, t, g: (b * n_g + g, 0, 0)))
        out_shape.append(jax.ShapeDtypeStruct((bsz * n_g, C, C), F32))
    return pl.pallas_call(
        functools.partial(_rwkv_kernel, L=L, emit_state=emit_state),
        grid=(bsz, n_t, n_g),
        in_specs=in_specs,
        out_specs=out_specs,
        out_shape=out_shape,
        scratch_shapes=[pltpu.VMEM((n_g, C, C), F32),
                        pltpu.VMEM((3 * n_g, 16, C), BF16),
                        pltpu.VMEM((16, lo_w), BF16)],
        compiler_params=_cparams(("parallel", "arbitrary", "arbitrary")),
        name="rwkv",
    )(proj, proj, proj, proj, carry16, carry16, carry16, carry16, s0,
      rp["mu_rkv"], rp["mu_rkv"], rp["mu_rkv"], rp["mu_lo"],
      rp["w0"], rp["a0"], rp["k_k"], rp["k_a"], rp["r_k"], rp["gn_w"], rp["gn_b"],
      rp["w_up"], rp["a_up"], rp["g_up"])


def _rope(x, c, s1, s2):
    return x * c + pltpu.roll(x, 96, axis=1) * s1 + pltpu.roll(x, 32, axis=1) * s2


def _mla_prep_kernel(cq_ref, ckv_ref, kpe_ref, qn_ref, kvn_ref, wq_ref, wkv_ref,
                     c_ref, s1_ref, s2_ref, q_ref, kn_ref, v_ref, kr_ref, *, n_heads):
    c, s1, s2 = c_ref[...], s1_ref[...], s2_ref[...]
    scale = QK_DIM ** -0.5 * np.log2(np.e)
    cq = _rms(cq_ref[...].astype(F32), qn_ref[...]).astype(BF16)
    for h in range(n_heads):
        qh = _dot(cq, wq_ref[:, h * QHEAD_COLS:(h + 1) * QHEAD_COLS]) * scale
        q_ref[:, h * QHEAD_COLS:h * QHEAD_COLS + LANES] = qh[:, :LANES].astype(BF16)
        q_ref[:, h * QHEAD_COLS + LANES:(h + 1) * QHEAD_COLS] = (
            _rope(qh[:, LANES:], c, s1, s2).astype(BF16))
    ckv = _rms(ckv_ref[...].astype(F32), kvn_ref[...]).astype(BF16)
    hw = n_heads * NOPE_DIM
    kn_ref[...] = _dot(ckv, wkv_ref[:, :hw]).astype(BF16)
    v_ref[...] = _dot(ckv, wkv_ref[:, hw:]).astype(BF16)
    kr_ref[...] = _rope(kpe_ref[...].astype(F32), c, s1, s2).astype(BF16)


def _mla_prep(proj, mp, tabs, *, seq, n_heads, off_cq, off_ckv, off_kpe):
    m = proj.shape[0]
    tm = min(512, seq)
    n_t = seq // tm
    ql = mp["w_uq"].shape[0]
    kl = mp["w_ukv"].shape[0]
    hw = n_heads * NOPE_DIM
    tab = pl.BlockSpec((tm, LANES), lambda i: (i % n_t, 0))
    full = lambda a: pl.BlockSpec(a.shape, lambda i: (0, 0))
    return pl.pallas_call(
        functools.partial(_mla_prep_kernel, n_heads=n_heads),
        grid=(m // tm,),
        in_specs=[pl.BlockSpec((tm, ql), lambda i: (i, off_cq // ql)),
                  pl.BlockSpec((tm, kl), lambda i: (i, off_ckv // kl)),
                  pl.BlockSpec((tm, LANES), lambda i: (i, off_kpe // LANES)),
                  full(mp["q_norm"]), full(mp["kv_norm"]), full(mp["w_uq"]), full(mp["w_ukv"]),
                  tab, tab, tab],
        out_specs=[pl.BlockSpec((tm, n_heads * QHEAD_COLS), lambda i: (i, 0)),
                   pl.BlockSpec((tm, hw), lambda i: (i, 0)),
                   pl.BlockSpec((tm, hw), lambda i: (i, 0)),
                   pl.BlockSpec((tm, LANES), lambda i: (i, 0))],
        out_shape=[jax.ShapeDtypeStruct((m, n_heads * QHEAD_COLS), BF16),
                   jax.ShapeDtypeStruct((m, hw), BF16),
                   jax.ShapeDtypeStruct((m, hw), BF16),
                   jax.ShapeDtypeStruct((m, LANES), BF16)],
        compiler_params=_cparams(("parallel",)),
        name="mla_prep",
    )(proj, proj, proj, mp["q_norm"], mp["kv_norm"], mp["w_uq"], mp["w_ukv"], *tabs)


def _attn_kernel(q_ref, kn_ref, kr_ref, v_ref, km_ref, vm_ref, o_ref, *, tq):
    seq = q_ref.shape[0]
    n_q = seq // tq
    qs = [q_ref[i * tq:(i + 1) * tq, :] for i in range(n_q)]
    kts = [jnp.concatenate([kn_ref[j * tq:(j + 1) * tq, :], kr_ref[j * tq:(j + 1) * tq, :]], axis=1)
           for j in range(n_q)]
    vts = [v_ref[j * tq:(j + 1) * tq, :] for j in range(n_q)]
    causal = _iota((tq, tq), 1) <= _iota((tq, tq), 0)

    ss = [_dot_nt(q, km_ref[0]) for q in qs]
    ms = [jnp.max(s, axis=-1, keepdims=True) for s in ss]
    ps = [jnp.exp2(s - m) for s, m in zip(ss, ms)]
    ls = [jnp.sum(p, axis=-1, keepdims=True) for p in ps]
    accs = [_dot(p.astype(BF16), vm_ref[0]) for p in ps]

    for d in range(n_q):
        act = list(range(d, n_q))
        ss = [_dot_nt(qs[i], kts[i - d]) for i in act]
        if d == 0:
            ss = [jnp.where(causal, s, -1e30) for s in ss]
        m_new = [jnp.maximum(ms[i], jnp.max(s, axis=-1, keepdims=True)) for i, s in zip(act, ss)]
        corr = [jnp.exp2(ms[i] - mn) for i, mn in zip(act, m_new)]
        ps = [jnp.exp2(s - mn) for s, mn in zip(ss, m_new)]
        pv = [_dot(p.astype(BF16), vts[i - d]) for i, p in zip(act, ps)]
        for n, i in enumerate(act):
            ms[i] = m_new[n]
            ls[i] = corr[n] * ls[i] + jnp.sum(ps[n], axis=-1, keepdims=True)
            accs[i] = corr[n] * accs[i] + pv[n]
    for i in range(n_q):
        o_ref[i * tq:(i + 1) * tq, :] = (accs[i] / ls[i]).astype(o_ref.dtype)


def _attn(q, kn, kr, v, k_meta, v_meta, *, bsz, seq, n_heads):
    tq = min(512, seq)
    return pl.pallas_call(
        functools.partial(_attn_kernel, tq=tq),
        grid=(bsz, n_heads),
        in_specs=[pl.BlockSpec((seq, QHEAD_COLS), lambda b, h: (b, h)),
                  pl.BlockSpec((seq, NOPE_DIM), lambda b, h: (b, h)),
                  pl.BlockSpec((seq, LANES), lambda b, h: (b, 0)),
                  pl.BlockSpec((seq, V_DIM), lambda b, h: (b, h)),
                  pl.BlockSpec((1, N_META, QHEAD_COLS), lambda b, h: (h, 0, 0)),
                  pl.BlockSpec((1, N_META, V_DIM), lambda b, h: (h, 0, 0))],
        out_specs=pl.BlockSpec((seq, V_DIM), lambda b, h: (b, h)),
        out_shape=jax.ShapeDtypeStruct((bsz * seq, n_heads * V_DIM), BF16),
        compiler_params=_cparams(("parallel", "parallel")),
        name="attn",
    )(q, kn, kr, v, k_meta, v_meta)


def _outproj_kernel(h_ref, ya_ref, yb_ref, ga_ref, gb_ref, w_ref, o_ref):
    mix = (_sigmoid(ga_ref[...].astype(F32)) * ya_ref[...].astype(F32)
           + _sigmoid(gb_ref[...].astype(F32)) * yb_ref[...].astype(F32))
    o_ref[...] = h_ref[...] + _dot(mix.astype(BF16), w_ref[...])


def _outproj(h, ya, yb, proj, w_out, *, off_gate):
    m, d = h.shape
    tm = min(512, m)
    gb = off_gate // d
    row = pl.BlockSpec((tm, d), lambda i: (i, 0))
    return pl.pallas_call(
        _outproj_kernel,
        grid=(m // tm,),
        in_specs=[row, row, row,
                  pl.BlockSpec((tm, d), lambda i: (i, gb)),
                  pl.BlockSpec((tm, d), lambda i: (i, gb + 1)),
                  pl.BlockSpec((d, d), lambda i: (0, 0))],
        out_specs=row,
        out_shape=jax.ShapeDtypeStruct((m, d), F32),
        compiler_params=_cparams(("parallel",)),
        name="outproj",
    )(h, ya, yb, proj, proj, w_out)


def _pad_cols(w, n):
    return jnp.pad(w, ((0, 0), (0, n - w.shape[1])))


def _pad_rows(w, n):
    return jnp.pad(w, ((0, n - w.shape[0]), (0, 0)))


def _rope_tables(pos):
    inv_freq = 1.0 / (ROPE_THETA ** (jnp.arange(0, ROPE_DIM, 2, dtype=F32) / ROPE_DIM))
    ang = pos.astype(F32)[:, None] * inv_freq[None, :]
    cos, sin = jnp.cos(ang), jnp.sin(ang)
    z = jnp.zeros_like(cos)
    return (jnp.concatenate([cos, cos, z, z], axis=1),
            jnp.concatenate([-sin, z, z, z], axis=1),
            jnp.concatenate([z, sin, z, z], axis=1))


def kernel(x, meta_tokens, ffn1_norm, ffn1_w_gate, ffn1_w_up, ffn1_w_down, mix_norm, w_in,
           tm_mu, w0, w_up, a0, a_up, g_up, k_k, k_a, r_k, gn_w, gn_b, q_norm, w_uq,
           kv_norm, w_ukv, w_out, ffn2_norm, ffn2_w_gate, ffn2_w_up, ffn2_w_down, final_norm):
    bsz, seq, d = x.shape
    assert d % GROUP_COLS == 0 and seq % 16 == 0
    assert ffn1_norm.shape[0] == 1, "single-layer stack"
    n_mla = d // 128
    w_lora, a_lora, g_lora = w_up.shape[1], a_up.shape[1], g_up.shape[1]
    q_lora, kv_lora = w_uq.shape[1], w_ukv.shape[1]
    assert w_lora <= LORA_PAD and a_lora <= LORA_PAD and g_lora == 2 * LORA_PAD
    rwkv_cols = 3 * d + w_lora + a_lora + g_lora
    mla_cols = q_lora + kv_lora + ROPE_DIM

    wi = w_in[0]
    o1, o2 = 3 * d, 3 * d + w_lora
    o3 = o2 + a_lora
    wi_rkv = wi[:, :3 * d]
    wi_lo = jnp.concatenate([_pad_cols(wi[:, o1:o2], LORA_PAD), _pad_cols(wi[:, o2:o3], LORA_PAD),
                             wi[:, o3:rwkv_cols]], axis=1)
    m0 = rwkv_cols
    wi_cq = wi[:, m0:m0 + q_lora]
    wi_ckv = wi[:, m0 + q_lora:m0 + q_lora + kv_lora]
    wi_kpe = _pad_cols(wi[:, m0 + q_lora + kv_lora:m0 + mla_cols], LANES)
    wi_gate = wi[:, m0 + mla_cols:]
    parts = [wi_gate, wi_rkv, wi_lo, wi_cq, wi_ckv, wi_kpe]
    offs = np.cumsum([0] + [p.shape[1] for p in parts])
    off_gate, off_rkv, off_lo, off_cq, off_ckv, off_kpe, n_used = (int(o) for o in offs)
    n_proj = -(-n_used // 1024) * 1024
    w_proj = _pad_cols(jnp.concatenate(parts, axis=1), n_proj).astype(BF16)

    mu = tm_mu[0]
    rp = dict(
        mu_rkv=mu[None, :3 * d],
        mu_lo=jnp.concatenate([jnp.pad(mu[o1:o2], (0, LORA_PAD - w_lora)),
                               jnp.pad(mu[o2:o3], (0, LORA_PAD - a_lora)), mu[o3:]])[None, :],
        w0=w0, a0=a0, k_k=k_k, k_a=k_a, r_k=r_k[0].reshape(1, d), gn_w=gn_w, gn_b=gn_b,
        w_up=_pad_rows(w_up[0], LORA_PAD).astype(BF16),
        a_up=_pad_rows(a_up[0], LORA_PAD).astype(BF16),
        g_up=g_up[0].astype(BF16),
    )

    wq = w_uq[0].reshape(q_lora, n_mla, QK_DIM)
    wq = jnp.pad(wq, ((0, 0), (0, 0), (0, QHEAD_COLS - QK_DIM))).reshape(q_lora, n_mla * QHEAD_COLS)
    wkv = w_ukv[0].reshape(kv_lora, n_mla, NOPE_DIM + V_DIM)
    wkv = jnp.concatenate([wkv[:, :, :NOPE_DIM].reshape(kv_lora, -1),
                           wkv[:, :, NOPE_DIM:].reshape(kv_lora, -1)], axis=1)
    mp = dict(q_norm=q_norm, kv_norm=kv_norm, w_uq=wq.astype(BF16), w_ukv=wkv.astype(BF16))

    f1 = (ffn1_w_gate[0].astype(BF16), ffn1_w_up[0].astype(BF16), ffn1_w_down[0].astype(BF16))
    f2 = (ffn2_w_gate[0].astype(BF16), ffn2_w_up[0].astype(BF16), ffn2_w_down[0].astype(BF16))
    w_o = w_out[0].astype(BF16)

    def front(h_rows, *, b, s, pos, carry16, s0, emit_state):
        h1, u = _ffn(h_rows, ffn1_norm, *f1, mix_norm, emit_h=True, norm_dtype=BF16)
        proj = _matmul(u, w_proj, BF16)
        rw = _rwkv(proj, carry16, s0, rp, bsz=b, seq=s, d=d, off_rkv=off_rkv, off_lo=off_lo,
                   emit_state=emit_state)
        q, kn, v, kr = _mla_prep(proj, mp, _rope_tables(pos), seq=s, n_heads=n_mla,
                                 off_cq=off_cq, off_ckv=off_ckv, off_kpe=off_kpe)
        return h1, proj, rw, (q, kn, kr, v)

    n_g = d // GROUP_COLS
    pad = META_ROWS - N_META
    meta_h = jnp.concatenate([jnp.zeros((pad, d), F32), meta_tokens.astype(F32)], axis=0)
    meta_pos = jnp.maximum(jnp.arange(META_ROWS) - pad, 0)
    _, proj_m, (_, s_meta), (_, kn_m, kr_m, v_m) = front(
        meta_h, b=1, s=META_ROWS, pos=meta_pos,
        carry16=jnp.zeros((16, n_proj), BF16), s0=jnp.zeros((n_g, GROUP_COLS, GROUP_COLS), F32),
        emit_state=True)
    kn_m = kn_m[pad:].reshape(N_META, n_mla, NOPE_DIM).transpose(1, 0, 2)
    kr_m = jnp.broadcast_to(kr_m[pad:][None], (n_mla, N_META, LANES))
    k_meta = jnp.concatenate([kn_m, kr_m], axis=2)
    v_meta = v_m[pad:].reshape(N_META, n_mla, V_DIM).transpose(1, 0, 2)

    h0 = x.reshape(bsz * seq, d)
    h1, proj, (ya,), (q, kn, kr, v) = front(
        h0, b=bsz, s=seq, pos=N_META + jnp.arange(seq),
        carry16=proj_m[META_ROWS - 16:], s0=s_meta, emit_state=False)
    yb = _attn(q, kn, kr, v, k_meta, v_meta, bsz=bsz, seq=seq, n_heads=n_mla)
    h2 = _outproj(h1, ya, yb, proj, w_o, off_gate=off_gate)
    (y,) = _ffn(h2, ffn2_norm, *f2, final_norm[None, :], emit_h=False, norm_dtype=F32)
    return y.reshape(bsz, seq, d)
```

```python
import functools

import numpy as np
import jax
import jax.numpy as jnp
from jax import lax
from jax.experimental import pallas as pl
from jax.experimental.pallas import tpu as pltpu

F32 = jnp.float32
BF16 = jnp.bfloat16

N_META = 16
NORM_EPS = 1e-6
RWKV_HEAD = 64
GN_EPS = RWKV_HEAD * 1e-5
NOPE_DIM = 128
ROPE_DIM = 64
V_DIM = 128
QK_DIM = NOPE_DIM + ROPE_DIM
ROPE_THETA = 10000.0

LANES = 128
GROUP_HEADS = 4
GROUP_COLS = GROUP_HEADS * RWKV_HEAD
CHUNK = 64
LORA_PAD = 128
QHEAD_COLS = 2 * LANES
META_ROWS = 64
VMEM_LIMIT = 56 * 1024 * 1024


def _cparams(sem):
    return pltpu.CompilerParams(dimension_semantics=sem, vmem_limit_bytes=VMEM_LIMIT)


def _sigmoid(x):
    return 1.0 / (1.0 + jnp.exp(-x))


def _dot(a, b):
    return jnp.dot(a, b, preferred_element_type=F32)


def _dot_nt(a, b):
    return lax.dot_general(a, b, (((1,), (1,)), ((), ())), preferred_element_type=F32)


def _dot_tn(a, b):
    return lax.dot_general(a, b, (((0,), (0,)), ((), ())), preferred_element_type=F32)


def _rms(x, g):
    ms = jnp.mean(x * x, axis=-1, keepdims=True)
    return x * lax.rsqrt(ms + NORM_EPS) * g


def _ffn_kernel(x_ref, g_ref, wg_ref, wu_ref, wd_ref, g2_ref, *rest, emit_h):
    if emit_h:
        h_ref, n_ref, xn_scr, acc_scr = rest
    else:
        n_ref, xn_scr, acc_scr = rest
    j = pl.program_id(1)

    @pl.when(j == 0)
    def _():
        xn_scr[...] = _rms(x_ref[...], g_ref[...]).astype(BF16)
        acc_scr[...] = jnp.zeros_like(acc_scr)

    xn = xn_scr[...]
    a = _dot(xn, wg_ref[...])
    b = _dot(xn, wu_ref[...])
    mid = (a * _sigmoid(a) * b).astype(BF16)
    acc_scr[...] += _dot(mid, wd_ref[...])

    @pl.when(j == pl.num_programs(1) - 1)
    def _():
        h = x_ref[...] + 0.5 * acc_scr[...]
        if emit_h:
            h_ref[...] = h
        n_ref[...] = _rms(h, g2_ref[...]).astype(n_ref.dtype)


def _ffn(x, g, wg, wu, wd, g2, *, emit_h, norm_dtype):
    m, d = x.shape
    dff = wg.shape[1]
    tm = min(512, m)
    tf = 512
    grid = (m // tm, dff // tf)
    row = pl.BlockSpec((tm, d), lambda i, j: (i, 0))
    vec = pl.BlockSpec((1, d), lambda i, j: (0, 0))
    out_shape = [jax.ShapeDtypeStruct((m, d), norm_dtype)]
    out_specs = [row]
    if emit_h:
        out_shape = [jax.ShapeDtypeStruct((m, d), F32)] + out_shape
        out_specs = [row] + out_specs
    return pl.pallas_call(
        functools.partial(_ffn_kernel, emit_h=emit_h),
        grid=grid,
        in_specs=[row, vec,
                  pl.BlockSpec((d, tf), lambda i, j: (0, j)),
                  pl.BlockSpec((d, tf), lambda i, j: (0, j)),
                  pl.BlockSpec((tf, d), lambda i, j: (j, 0)),
                  vec],
        out_specs=out_specs,
        out_shape=out_shape,
        scratch_shapes=[pltpu.VMEM((tm, d), BF16), pltpu.VMEM((tm, d), F32)],
        compiler_params=_cparams(("parallel", "arbitrary")),
        name="ffn",
    )(x, g, wg, wu, wd, g2)


def _matmul_kernel(x_ref, w_ref, o_ref):
    o_ref[...] = _dot(x_ref[...], w_ref[...]).astype(o_ref.dtype)


def _matmul(x, w, out_dtype):
    m, k = x.shape
    n = w.shape[1]
    tm = min(1024, m)
    tn = 1024
    return pl.pallas_call(
        _matmul_kernel,
        grid=(m // tm, n // tn),
        in_specs=[pl.BlockSpec((tm, k), lambda i, j: (i, 0)),
                  pl.BlockSpec((k, tn), lambda i, j: (0, j))],
        out_specs=pl.BlockSpec((tm, tn), lambda i, j: (i, j)),
        out_shape=jax.ShapeDtypeStruct((m, n), out_dtype),
        compiler_params=_cparams(("parallel", "parallel")),
        name="inproj",
    )(x, w)


def _iota(shape, dim):
    return lax.broadcasted_iota(jnp.int32, shape, dim)


def _split(x, parts):
    out = []
    for _ in range(parts - 1):
        hi = x.astype(BF16)
        out.append(hi)
        x = x - hi.astype(F32)
    out.append(x.astype(BF16))
    return out


def _dot01(m01, x, parts):
    return sum(_dot(m01, xp) for xp in _split(x, parts))


def _dot01_r(x, m01, parts):
    return sum(_dot(xp, m01) for xp in _split(x, parts))


def _wkv_masks(L, tt):
    C, H = GROUP_COLS, GROUP_HEADS
    hl = H * L
    r_hl, c_c = np.arange(hl)[:, None], np.arange(C)[None, :]
    t, s = np.arange(L)[:, None], np.arange(hl)[None, :] % L
    rt, ct = np.arange(tt)[:, None], np.arange(tt)[None, :]
    return dict(
        bd_lc=jnp.asarray(r_hl // L == c_c // RWKV_HEAD, BF16),
        bd_ll=jnp.asarray(r_hl // L == np.arange(hl)[None, :] // L, BF16),
        bd_cc=jnp.asarray(np.arange(C)[:, None] // RWKV_HEAD == c_c // RWKV_HEAD, F32),
        seg=jnp.asarray(np.arange(C)[:, None] // RWKV_HEAD == c_c // RWKV_HEAD, BF16),
        strict=jnp.asarray(t > s, F32),
        incl=jnp.asarray(t >= s, F32),
        eye=jnp.asarray(t == s, F32),
        tri=jnp.asarray((rt >= ct) & (rt // L == ct // L), BF16),
    )


_MASK_NAMES = ("bd_lc", "bd_ll", "bd_cc", "seg", "strict", "incl", "eye", "tri")
_TAIL_NAMES = ("wl", "bonus", "gate")
_SLOT1_NAMES = ("ar", "bk", "vbd", "abd", "bkh", "v") + _TAIL_NAMES
_SLOT2_NAMES = ("q", "y0", "g", "n") + _TAIL_NAMES


def _slot_shapes(tt, L):
    C, n_c, hl = GROUP_COLS, tt // L, GROUP_HEADS * L
    return dict(ar=((n_c, 2 * L, C), BF16), bk=((n_c, 2 * hl, C), BF16), vbd=((n_c, hl, C), BF16),
                abd=((n_c, hl, C), BF16), bkh=((n_c, 2 * L, C), BF16), v=((tt, C), BF16),
                wl=((n_c, 1, C), F32), bonus=((tt, C), F32), gate=((tt, C), F32),
                q=((n_c, L, C), BF16), y0=((n_c, L, C), F32), g=((n_c, C, C), BF16),
                n=((n_c, C, C), F32))


def _shift_rows(x, carry_row):
    prev = pltpu.roll(x, 1, axis=0)
    return jnp.where(_iota(x.shape, 0) == 0, carry_row, prev)


def _bf(x):
    return x.astype(BF16)


def _cat(axis, *xs):
    return jnp.concatenate(xs, axis=axis)


def _tile_rows(x):
    return jnp.concatenate([x] * GROUP_HEADS, axis=0)


def _each(f, *lists):
    return [f(*xs) for xs in zip(*lists)]


def _rwkv_prep(tp, gp, n_g, L, mk, refs, wr):
    (pr_ref, pk_ref, pv_ref, plo_ref, cr_ref, ck_ref, cv_ref, clo_ref,
     mur_ref, muk_ref, muv_ref, mulo_ref, w0_ref, a0_ref, kk_ref, ka_ref, rk_ref,
     wup_ref, aup_ref, gup_ref, crkv_scr, clo_scr) = refs
    tt = pr_ref.shape[0]
    first = tp == 0

    def shifted(p_ref, init_ref, carry16, mu):
        x = p_ref[...].astype(F32)
        carry = jnp.where(first, init_ref[...], carry16)
        prev = _shift_rows(x, carry[15:16, :].astype(F32))
        return x + mu * (prev - x)

    r = shifted(pr_ref, cr_ref, crkv_scr[3 * gp + 0], mur_ref[...])
    k = shifted(pk_ref, ck_ref, crkv_scr[3 * gp + 1], muk_ref[...])
    v = shifted(pv_ref, cv_ref, crkv_scr[3 * gp + 2], muv_ref[...])
    lo_carry = clo_scr[...]
    lo = shifted(plo_ref, clo_ref, lo_carry, mulo_ref[...])
    crkv_scr[3 * gp + 0] = pr_ref[tt - 16:tt, :]
    crkv_scr[3 * gp + 1] = pk_ref[tt - 16:tt, :]
    crkv_scr[3 * gp + 2] = pv_ref[tt - 16:tt, :]
    clo_scr[...] = jnp.where(gp == n_g - 1, plo_ref[tt - 16:tt, :], lo_carry)
    xw = lo[:, 0:LORA_PAD]
    xa = lo[:, LORA_PAD:2 * LORA_PAD]
    xg = lo[:, 2 * LORA_PAD:]
    kkr = k * kk_ref[...]
    yield

    z = -(w0_ref[...] + _dot(_bf(jnp.tanh(xw)), wup_ref[...]))
    alpha = _sigmoid(a0_ref[...] + _dot(_bf(xa), aup_ref[...]))
    wr["gate"][...] = _dot(_bf(_sigmoid(xg)), gup_ref[...])
    kk = kkr * lax.rsqrt(jnp.maximum(_dot01_r(kkr * kkr, mk["seg"][...], 2), 1e-24))
    softplus = jnp.maximum(z, 0.0) + jnp.log1p(jnp.exp(-jnp.abs(z)))
    lw = -jnp.exp(-softplus - 0.5)
    k2 = k * (1.0 + (alpha - 1.0) * ka_ref[...])
    b = kk * alpha
    yield

    cum = _dot01(mk["tri"][...], lw, 3)
    wr["bonus"][...] = _dot01_r(r * k2 * rk_ref[...], mk["seg"][...], 2) * v
    e_n = jnp.exp(-cum)
    a_t = _bf(-kk * jnp.exp(cum - lw))
    r_t = _bf(r * jnp.exp(cum))
    b_t = _bf(b * e_n)
    k_t = _bf(k2 * e_n)
    vb = _bf(v)
    wr["v"][...] = vb
    bd_lc = mk["bd_lc"][...]
    for c in range(tt // L):
        sl = slice(c * L, (c + 1) * L)
        cl = cum[sl.stop - 1:sl.stop, :]
        e_l = jnp.exp(cl - cum[sl])
        wr["wl"][c] = jnp.exp(cl)
        wr["ar"][c, :L] = a_t[sl]
        wr["ar"][c, L:] = r_t[sl]
        wr["bk"][c, :GROUP_HEADS * L] = _tile_rows(b_t[sl]) * bd_lc
        wr["bk"][c, GROUP_HEADS * L:] = _tile_rows(k_t[sl]) * bd_lc
        wr["vbd"][c] = _tile_rows(vb[sl]) * bd_lc
        wr["abd"][c] = _tile_rows(a_t[sl]) * bd_lc
        wr["bkh"][c, :L] = _bf(b[sl] * e_l)
        wr["bkh"][c, L:] = _bf(k2[sl] * e_l)


def _rwkv_local(L, mk, rd, wr):
    tt = rd["v"].shape[0]
    n_c = tt // L
    n_h = GROUP_HEADS * L
    C = GROUP_COLS
    chunks = [slice(c * L, (c + 1) * L) for c in range(n_c)]
    bd_lc, bd_ll = mk["bd_lc"][...], mk["bd_ll"][...]
    strict, incl = mk["strict"][...] > 0, mk["incl"][...] > 0
    bd_cc = mk["bd_cc"][...] > 0
    bd = lambda xb: _tile_rows(xb) * bd_lc
    bd_sq = lambda pb: _tile_rows(pb) * bd_ll
    for k in _TAIL_NAMES:
        wr[k][...] = rd[k][...]

    ar = [rd["ar"][c] for c in range(n_c)]
    aa = _each(_dot_nt, ar, [rd["bk"][c] for c in range(n_c)])
    yield
    a_ab = _each(lambda x: jnp.where(strict, x[:L, :n_h], 0.0), aa)
    a_ak = _each(lambda x: _bf(jnp.where(strict, x[:L, n_h:], 0.0)), aa)
    a_rb = _each(lambda x: _bf(jnp.where(incl, x[L:, :n_h], 0.0)), aa)
    a_rk = _each(lambda x: _bf(jnp.where(incl, x[L:, n_h:], 0.0)), aa)

    n_steps = int(np.log2(L))
    eye = mk["eye"][...]
    tp = _each(lambda x: eye + x, a_ab)
    p = _each(lambda x: _dot(_bf(x), bd_sq(_bf(x))), a_ab)
    yield
    for _ in range(1, n_steps - 1):
        out = _each(lambda x, t: _dot(_bf(_cat(0, x, t)), bd_sq(_bf(x))), p, tp)
        yield
        p = _each(lambda o: o[:L], out)
        tp = _each(lambda t, o: t + o[L:], tp, out)
    tb = _each(lambda t, x: _bf(t + _dot(_bf(t), bd_sq(_bf(x)))), tp, p)
    yield

    v_bd = [rd["vbd"][c] for c in range(n_c)]
    x0 = _each(_dot, a_ak, v_bd)
    yield
    ut = _each(lambda t, x, c: _dot(t, _cat(1, bd(_bf(x)), rd["abd"][c])), tb, x0, range(n_c))
    yield
    u0 = _each(lambda x: _bf(x[:, :C]), ut)
    ta = _each(lambda x: _bf(x[:, C:]), ut)
    for c in range(n_c):
        wr["q"][c] = _bf(ar[c][L:].astype(F32) + _dot(a_rb[c], bd(ta[c])))
    yield
    for c in range(n_c):
        wr["y0"][c] = _dot(_cat(1, a_rb[c], a_rk[c]), _cat(0, bd(u0[c]), v_bd[c]))
    yield
    for c in range(n_c):
        bkh = rd["bkh"][c]
        wr["g"][c] = _bf(jnp.where(bd_cc, _dot_tn(ta[c], bkh[:L]), 0.0))
        wr["n"][c] = jnp.where(bd_cc, _dot_tn(_cat(0, u0[c], rd["v"][chunks[c]]), bkh), 0.0)


def _rwkv_serial(tc, gc, L, mk, refs, rd, emit_state):
    s0_ref, gnw_ref, gnb_ref, y_ref, sout_ref, s_scr = refs
    n_c = rd["q"].shape[0]
    S = jnp.where(tc == 0, s0_ref[gc], s_scr[gc])
    ys = []
    for c in range(n_c):
        sb = _bf(S)
        ys.append(rd["y0"][c] + _dot_nt(rd["q"][c], sb))
        S = S * rd["wl"][c] + _dot(sb, rd["g"][c]) + rd["n"][c]
        yield
    s_scr[gc] = S
    if emit_state:
        sout_ref[0] = S
    y = ys[0] if n_c == 1 else _cat(0, *ys)

    inv_n = 1.0 / RWKV_HEAD
    seg = mk["seg"][...]
    mean = _dot01_r(y, seg, 2) * inv_n
    yield
    d = y - mean
    var = _dot01_r(d * d, seg, 2) * inv_n
    yield
    yn = d * lax.rsqrt(var + GN_EPS) * gnw_ref[...] + gnb_ref[...]
    y_ref[...] = ((yn + rd["bonus"][...]) * rd["gate"][...]).astype(y_ref.dtype)


def _weave(gens, order):
    for who in order:
        next(gens[who], None)
    for gen in gens.values():
        for _ in gen:
            pass


_WEAVE = "pls" + "ls" * 2 + "p" + "ls" * 4 + "p" + "ls" * 8


def _rwkv_kernel(*refs, L, n_t, n_g, emit_state):
    n_in = 20 + len(_MASK_NAMES) + 3
    in_refs, rest = refs[:n_in], refs[n_in:]
    prep_in, masks, chain_in = in_refs[:20], in_refs[20:20 + len(_MASK_NAMES)], in_refs[-3:]
    mk = dict(zip(_MASK_NAMES, masks))
    if emit_state:
        y_ref, sout_ref = rest[:2]
        scr = rest[2:]
    else:
        y_ref, sout_ref = rest[0], None
        scr = rest[1:]
    s_scr, crkv_scr, clo_scr = scr[:3]
    scr = list(scr[3:])
    slots1 = [dict(zip(_SLOT1_NAMES, [scr.pop(0) for _ in _SLOT1_NAMES])) for _ in range(2)]
    slots2 = [dict(zip(_SLOT2_NAMES, [scr.pop(0) for _ in _SLOT2_NAMES])) for _ in range(2)]

    s = pl.program_id(0)
    n = pl.num_programs(0) - 2
    sp = jnp.minimum(s, n - 1)
    sc = jnp.clip(s - 2, 0, n - 1)
    tp, gp = (sp // n_g) % n_t, sp % n_g
    tc, gc = (sc // n_g) % n_t, sc % n_g

    @pl.when(s == 0)
    def _():
        for ref in (tuple(slots1[1].values()) + tuple(slots2[0].values()) + tuple(slots2[1].values())
                    + (s_scr, crkv_scr, clo_scr)):
            ref[...] = jnp.zeros_like(ref)

    def step(parity):
        gens = dict(
            p=_rwkv_prep(tp, gp, n_g, L, mk, prep_in + (crkv_scr, clo_scr), slots1[parity]),
            l=_rwkv_local(L, mk, slots1[1 - parity], slots2[1 - parity]),
            s=_rwkv_serial(tc, gc, L, mk, chain_in + (y_ref, sout_ref, s_scr), slots2[parity],
                           emit_state))
        _weave(gens, _WEAVE)

    for parity in range(2):
        pl.when(s % 2 == parity)(functools.partial(step, parity))


def _rwkv(proj, carry16, s0, rp, *, bsz, seq, d, off_rkv, off_lo, emit_state):
    C = GROUP_COLS
    n_g = d // C
    tt = min(512, seq)
    L = min(CHUNK, tt)
    n_t = seq // tt
    assert n_t == 1 or not emit_state
    n = bsz * n_t * n_g
    lo_w = 4 * LORA_PAD
    cb = off_rkv // C
    lb = off_lo // lo_w
    masks = _wkv_masks(L, tt)

    def tile(s):
        return s // n_g, s % n_g

    def prep_ix(f):
        return lambda s: f(*tile(jnp.minimum(s, n - 1)))

    def serial_ix(f):
        return lambda s: f(*tile(jnp.clip(s - 2, 0, n - 1)))

    pblk = lambda j: pl.BlockSpec((tt, C), prep_ix(lambda r, g: (r, cb + j * n_g + g)))
    cblk = lambda j: pl.BlockSpec((16, C), prep_ix(lambda r, g: (0, cb + j * n_g + g)))
    vblk = lambda j=0: pl.BlockSpec((1, C), prep_ix(lambda r, g: (0, j * n_g + g)))
    const = lambda a: pl.BlockSpec(a.shape, lambda s, nd=a.ndim: (0,) * nd)
    in_specs = [
        pblk(0), pblk(1), pblk(2),
        pl.BlockSpec((tt, lo_w), prep_ix(lambda r, g: (r, lb))),
        cblk(0), cblk(1), cblk(2),
        pl.BlockSpec((16, lo_w), lambda s: (0, lb)),
        vblk(0), vblk(1), vblk(2),
        pl.BlockSpec((1, lo_w), lambda s: (0, 0)),
        vblk(), vblk(), vblk(), vblk(), vblk(),
        pl.BlockSpec((LORA_PAD, C), prep_ix(lambda r, g: (0, g))),
        pl.BlockSpec((LORA_PAD, C), prep_ix(lambda r, g: (0, g))),
        pl.BlockSpec((2 * LORA_PAD, C), prep_ix(lambda r, g: (0, g))),
    ] + [const(masks[k]) for k in _MASK_NAMES] + [
        const(s0),
        pl.BlockSpec((1, C), serial_ix(lambda r, g: (0, g))),
        pl.BlockSpec((1, C), serial_ix(lambda r, g: (0, g))),
    ]
    out_specs = [pl.BlockSpec((tt, C), serial_ix(lambda r, g: (r, g)))]
    out_shape = [jax.ShapeDtypeStruct((bsz * seq, d), BF16)]
    if emit_state:
        out_specs.append(pl.BlockSpec((1, C, C), serial_ix(lambda r, g: (r * n_g + g, 0, 0))))
        out_shape.append(jax.ShapeDtypeStruct((bsz * n_g, C, C), F32))
    shp = _slot_shapes(tt, L)
    scratch = [pltpu.VMEM((n_g, C, C), F32), pltpu.VMEM((3 * n_g, 16, C), BF16),
               pltpu.VMEM((16, lo_w), BF16)]
    scratch += [pltpu.VMEM(*shp[k]) for _ in range(2) for k in _SLOT1_NAMES]
    scratch += [pltpu.VMEM(*shp[k]) for _ in range(2) for k in _SLOT2_NAMES]
    return pl.pallas_call(
        functools.partial(_rwkv_kernel, L=L, n_t=n_t, n_g=n_g, emit_state=emit_state),
        grid=(n + 2,),
        in_specs=in_specs,
        out_specs=out_specs,
        out_shape=out_shape,
        scratch_shapes=scratch,
        compiler_params=_cparams(("arbitrary",)),
        name="rwkv",
    )(proj, proj, proj, proj, carry16, carry16, carry16, carry16,
      rp["mu_rkv"], rp["mu_rkv"], rp["mu_rkv"], rp["mu_lo"],
      rp["w0"], rp["a0"], rp["k_k"], rp["k_a"], rp["r_k"],
      rp["w_up"], rp["a_up"], rp["g_up"],
      *[masks[k] for k in _MASK_NAMES], s0, rp["gn_w"], rp["gn_b"])


def _rope(x, c, s1, s2):
    return x * c + pltpu.roll(x, 96, axis=1) * s1 + pltpu.roll(x, 32, axis=1) * s2


def _mla_prep_kernel(cq_ref, ckv_ref, kpe_ref, qn_ref, kvn_ref, wq_ref, wkv_ref,
                     c_ref, s1_ref, s2_ref, q_ref, kn_ref, v_ref, kr_ref, *, n_heads):
    c, s1, s2 = c_ref[...], s1_ref[...], s2_ref[...]
    scale = QK_DIM ** -0.5 * np.log2(np.e)
    cq = _rms(cq_ref[...].astype(F32), qn_ref[...]).astype(BF16)
    for h in range(n_heads):
        qh = _dot(cq, wq_ref[:, h * QHEAD_COLS:(h + 1) * QHEAD_COLS]) * scale
        q_ref[:, h * QHEAD_COLS:h * QHEAD_COLS + LANES] = qh[:, :LANES].astype(BF16)
        q_ref[:, h * QHEAD_COLS + LANES:(h + 1) * QHEAD_COLS] = (
            _rope(qh[:, LANES:], c, s1, s2).astype(BF16))
    ckv = _rms(ckv_ref[...].astype(F32), kvn_ref[...]).astype(BF16)
    hw = n_heads * NOPE_DIM
    kn_ref[...] = _dot(ckv, wkv_ref[:, :hw]).astype(BF16)
    v_ref[...] = _dot(ckv, wkv_ref[:, hw:]).astype(BF16)
    kr_ref[...] = _rope(kpe_ref[...].astype(F32), c, s1, s2).astype(BF16)


def _mla_prep(proj, mp, tabs, *, seq, n_heads, off_cq, off_ckv, off_kpe):
    m = proj.shape[0]
    tm = min(512, seq)
    n_t = seq // tm
    ql = mp["w_uq"].shape[0]
    kl = mp["w_ukv"].shape[0]
    hw = n_heads * NOPE_DIM
    tab = pl.BlockSpec((tm, LANES), lambda i: (i % n_t, 0))
    full = lambda a: pl.BlockSpec(a.shape, lambda i: (0, 0))
    return pl.pallas_call(
        functools.partial(_mla_prep_kernel, n_heads=n_heads),
        grid=(m // tm,),
        in_specs=[pl.BlockSpec((tm, ql), lambda i: (i, off_cq // ql)),
                  pl.BlockSpec((tm, kl), lambda i: (i, off_ckv // kl)),
                  pl.BlockSpec((tm, LANES), lambda i: (i, off_kpe // LANES)),
                  full(mp["q_norm"]), full(mp["kv_norm"]), full(mp["w_uq"]), full(mp["w_ukv"]),
                  tab, tab, tab],
        out_specs=[pl.BlockSpec((tm, n_heads * QHEAD_COLS), lambda i: (i, 0)),
                   pl.BlockSpec((tm, hw), lambda i: (i, 0)),
                   pl.BlockSpec((tm, hw), lambda i: (i, 0)),
                   pl.BlockSpec((tm, LANES), lambda i: (i, 0))],
        out_shape=[jax.ShapeDtypeStruct((m, n_heads * QHEAD_COLS), BF16),
                   jax.ShapeDtypeStruct((m, hw), BF16),
                   jax.ShapeDtypeStruct((m, hw), BF16),
                   jax.ShapeDtypeStruct((m, LANES), BF16)],
        compiler_params=_cparams(("parallel",)),
        name="mla_prep",
    )(proj, proj, proj, mp["q_norm"], mp["kv_norm"], mp["w_uq"], mp["w_ukv"], *tabs)


def _attn_kernel(q_ref, kn_ref, kr_ref, v_ref, km_ref, vm_ref, o_ref, *, tq):
    seq = q_ref.shape[0]
    n_q = seq // tq
    qs = [q_ref[i * tq:(i + 1) * tq, :] for i in range(n_q)]
    kts = [jnp.concatenate([kn_ref[j * tq:(j + 1) * tq, :], kr_ref[j * tq:(j + 1) * tq, :]], axis=1)
           for j in range(n_q)]
    vts = [v_ref[j * tq:(j + 1) * tq, :] for j in range(n_q)]
    causal = _iota((tq, tq), 1) <= _iota((tq, tq), 0)

    ss = [_dot_nt(q, km_ref[0]) for q in qs]
    ms = [jnp.max(s, axis=-1, keepdims=True) for s in ss]
    ps = [jnp.exp2(s - m) for s, m in zip(ss, ms)]
    ls = [jnp.sum(p, axis=-1, keepdims=True) for p in ps]
    accs = [_dot(p.astype(BF16), vm_ref[0]) for p in ps]

    for d in range(n_q):
        act = list(range(d, n_q))
        ss = [_dot_nt(qs[i], kts[i - d]) for i in act]
        if d == 0:
            ss = [jnp.where(causal, s, -1e30) for s in ss]
        m_new = [jnp.maximum(ms[i], jnp.max(s, axis=-1, keepdims=True)) for i, s in zip(act, ss)]
        corr = [jnp.exp2(ms[i] - mn) for i, mn in zip(act, m_new)]
        ps = [jnp.exp2(s - mn) for s, mn in zip(ss, m_new)]
        pv = [_dot(p.astype(BF16), vts[i - d]) for i, p in zip(act, ps)]
        for n, i in enumerate(act):
            ms[i] = m_new[n]
            ls[i] = corr[n] * ls[i] + jnp.sum(ps[n], axis=-1, keepdims=True)
            accs[i] = corr[n] * accs[i] + pv[n]
    for i in range(n_q):
        o_ref[i * tq:(i + 1) * tq, :] = (accs[i] / ls[i]).astype(o_ref.dtype)


def _attn(q, kn, kr, v, k_meta, v_meta, *, bsz, seq, n_heads):
    tq = min(512, seq)
    return pl.pallas_call(
        functools.partial(_attn_kernel, tq=tq),
        grid=(bsz, n_heads),
        in_specs=[pl.BlockSpec((seq, QHEAD_COLS), lambda b, h: (b, h)),
                  pl.BlockSpec((seq, NOPE_DIM), lambda b, h: (b, h)),
                  pl.BlockSpec((seq, LANES), lambda b, h: (b, 0)),
                  pl.BlockSpec((seq, V_DIM), lambda b, h: (b, h)),
                  pl.BlockSpec((1, N_META, QHEAD_COLS), lambda b, h: (h, 0, 0)),
                  pl.BlockSpec((1, N_META, V_DIM), lambda b, h: (h, 0, 0))],
        out_specs=pl.BlockSpec((seq, V_DIM), lambda b, h: (b, h)),
        out_shape=jax.ShapeDtypeStruct((bsz * seq, n_heads * V_DIM), BF16),
        compiler_params=_cparams(("parallel", "parallel")),
        name="attn",
    )(q, kn, kr, v, k_meta, v_meta)


def _outproj_kernel(h_ref, ya_ref, yb_ref, ga_ref, gb_ref, w_ref, o_ref):
    mix = (_sigmoid(ga_ref[...].astype(F32)) * ya_ref[...].astype(F32)
           + _sigmoid(gb_ref[...].astype(F32)) * yb_ref[...].astype(F32))
    o_ref[...] = h_ref[...] + _dot(mix.astype(BF16), w_ref[...])


def _outproj(h, ya, yb, proj, w_out, *, off_gate):
    m, d = h.shape
    tm = min(512, m)
    gb = off_gate // d
    row = pl.BlockSpec((tm, d), lambda i: (i, 0))
    return pl.pallas_call(
        _outproj_kernel,
        grid=(m // tm,),
        in_specs=[row, row, row,
                  pl.BlockSpec((tm, d), lambda i: (i, gb)),
                  pl.BlockSpec((tm, d), lambda i: (i, gb + 1)),
                  pl.BlockSpec((d, d), lambda i: (0, 0))],
        out_specs=row,
        out_shape=jax.ShapeDtypeStruct((m, d), F32),
        compiler_params=_cparams(("parallel",)),
        name="outproj",
    )(h, ya, yb, proj, proj, w_out)


def _pad_cols(w, n):
    return jnp.pad(w, ((0, 0), (0, n - w.shape[1])))


def _pad_rows(w, n):
    return jnp.pad(w, ((0, n - w.shape[0]), (0, 0)))


def _rope_tables(pos):
    inv_freq = 1.0 / (ROPE_THETA ** (jnp.arange(0, ROPE_DIM, 2, dtype=F32) / ROPE_DIM))
    ang = pos.astype(F32)[:, None] * inv_freq[None, :]
    cos, sin = jnp.cos(ang), jnp.sin(ang)
    z = jnp.zeros_like(cos)
    return (jnp.concatenate([cos, cos, z, z], axis=1),
            jnp.concatenate([-sin, z, z, z], axis=1),
            jnp.concatenate([z, sin, z, z], axis=1))


def kernel(x, meta_tokens, ffn1_norm, ffn1_w_gate, ffn1_w_up, ffn1_w_down, mix_norm, w_in,
           tm_mu, w0, w_up, a0, a_up, g_up, k_k, k_a, r_k, gn_w, gn_b, q_norm, w_uq,
           kv_norm, w_ukv, w_out, ffn2_norm, ffn2_w_gate, ffn2_w_up, ffn2_w_down, final_norm):
    bsz, seq, d = x.shape
    assert d % GROUP_COLS == 0 and seq % 16 == 0
    assert ffn1_norm.shape[0] == 1, "single-layer stack"
    n_mla = d // 128
    w_lora, a_lora, g_lora = w_up.shape[1], a_up.shape[1], g_up.shape[1]
    q_lora, kv_lora = w_uq.shape[1], w_ukv.shape[1]
    assert w_lora <= LORA_PAD and a_lora <= LORA_PAD and g_lora == 2 * LORA_PAD
    rwkv_cols = 3 * d + w_lora + a_lora + g_lora
    mla_cols = q_lora + kv_lora + ROPE_DIM

    wi = w_in[0]
    o1, o2 = 3 * d, 3 * d + w_lora
    o3 = o2 + a_lora
    wi_rkv = wi[:, :3 * d]
    wi_lo = jnp.concatenate([_pad_cols(wi[:, o1:o2], LORA_PAD), _pad_cols(wi[:, o2:o3], LORA_PAD),
                             wi[:, o3:rwkv_cols]], axis=1)
    m0 = rwkv_cols
    wi_cq = wi[:, m0:m0 + q_lora]
    wi_ckv = wi[:, m0 + q_lora:m0 + q_lora + kv_lora]
    wi_kpe = _pad_cols(wi[:, m0 + q_lora + kv_lora:m0 + mla_cols], LANES)
    wi_gate = wi[:, m0 + mla_cols:]
    parts = [wi_gate, wi_rkv, wi_lo, wi_cq, wi_ckv, wi_kpe]
    offs = np.cumsum([0] + [p.shape[1] for p in parts])
    off_gate, off_rkv, off_lo, off_cq, off_ckv, off_kpe, n_used = (int(o) for o in offs)
    n_proj = -(-n_used // 1024) * 1024
    w_proj = _pad_cols(jnp.concatenate(parts, axis=1), n_proj).astype(BF16)

    mu = tm_mu[0]
    rp = dict(
        mu_rkv=mu[None, :3 * d],
        mu_lo=jnp.concatenate([jnp.pad(mu[o1:o2], (0, LORA_PAD - w_lora)),
                               jnp.pad(mu[o2:o3], (0, LORA_PAD - a_lora)), mu[o3:]])[None, :],
        w0=w0, a0=a0, k_k=k_k, k_a=k_a, r_k=r_k[0].reshape(1, d), gn_w=gn_w, gn_b=gn_b,
        w_up=_pad_rows(w_up[0], LORA_PAD).astype(BF16),
        a_up=_pad_rows(a_up[0], LORA_PAD).astype(BF16),
        g_up=g_up[0].astype(BF16),
    )

    wq = w_uq[0].reshape(q_lora, n_mla, QK_DIM)
    wq = jnp.pad(wq, ((0, 0), (0, 0), (0, QHEAD_COLS - QK_DIM))).reshape(q_lora, n_mla * QHEAD_COLS)
    wkv = w_ukv[0].reshape(kv_lora, n_mla, NOPE_DIM + V_DIM)
    wkv = jnp.concatenate([wkv[:, :, :NOPE_DIM].reshape(kv_lora, -1),
                           wkv[:, :, NOPE_DIM:].reshape(kv_lora, -1)], axis=1)
    mp = dict(q_norm=q_norm, kv_norm=kv_norm, w_uq=wq.astype(BF16), w_ukv=wkv.astype(BF16))

    f1 = (ffn1_w_gate[0].astype(BF16), ffn1_w_up[0].astype(BF16), ffn1_w_down[0].astype(BF16))
    f2 = (ffn2_w_gate[0].astype(BF16), ffn2_w_up[0].astype(BF16), ffn2_w_down[0].astype(BF16))
    w_o = w_out[0].astype(BF16)

    def front(h_rows, *, b, s, pos, carry16, s0, emit_state):
        h1, u = _ffn(h_rows, ffn1_norm, *f1, mix_norm, emit_h=True, norm_dtype=BF16)
        proj = _matmul(u, w_proj, BF16)
        rw = _rwkv(proj, carry16, s0, rp, bsz=b, seq=s, d=d, off_rkv=off_rkv, off_lo=off_lo,
                   emit_state=emit_state)
        q, kn, v, kr = _mla_prep(proj, mp, _rope_tables(pos), seq=s, n_heads=n_mla,
                                 off_cq=off_cq, off_ckv=off_ckv, off_kpe=off_kpe)
        return h1, proj, rw, (q, kn, kr, v)

    n_g = d // GROUP_COLS
    pad = META_ROWS - N_META
    meta_h = jnp.concatenate([jnp.zeros((pad, d), F32), meta_tokens.astype(F32)], axis=0)
    meta_pos = jnp.maximum(jnp.arange(META_ROWS) - pad, 0)
    _, proj_m, (_, s_meta), (_, kn_m, kr_m, v_m) = front(
        meta_h, b=1, s=META_ROWS, pos=meta_pos,
        carry16=jnp.zeros((16, n_proj), BF16), s0=jnp.zeros((n_g, GROUP_COLS, GROUP_COLS), F32),
        emit_state=True)
    kn_m = kn_m[pad:].reshape(N_META, n_mla, NOPE_DIM).transpose(1, 0, 2)
    kr_m = jnp.broadcast_to(kr_m[pad:][None], (n_mla, N_META, LANES))
    k_meta = jnp.concatenate([kn_m, kr_m], axis=2)
    v_meta = v_m[pad:].reshape(N_META, n_mla, V_DIM).transpose(1, 0, 2)

    h0 = x.reshape(bsz * seq, d)
    h1, proj, (ya,), (q, kn, kr, v) = front(
        h0, b=bsz, s=seq, pos=N_META + jnp.arange(seq),
        carry16=proj_m[META_ROWS - 16:], s0=s_meta, emit_state=False)
    yb = _attn(q, kn, kr, v, k_meta, v_meta, bsz=bsz, seq=seq, n_heads=n_mla)
    h2 = _outproj(h1, ya, yb, proj, w_o, off_gate=off_gate)
    (y,) = _ffn(h2, ffn2_norm, *f2, final_norm[None, :], emit_h=False, norm_dtype=F32)
    return y.reshape(bsz, seq, d)
```

```python
import functools

import numpy as np
import jax
import jax.numpy as jnp
from jax import lax
from jax.experimental import pallas as pl
from jax.experimental.pallas import tpu as pltpu

F32 = jnp.float32
BF16 = jnp.bfloat16

N_META = 16
NORM_EPS = 1e-6
RWKV_HEAD = 64
GN_EPS = RWKV_HEAD * 1e-5
NOPE_DIM = 128
ROPE_DIM = 64
V_DIM = 128
QK_DIM = NOPE_DIM + ROPE_DIM
ROPE_THETA = 10000.0

LANES = 128
GROUP_HEADS = 4
GROUP_COLS = GROUP_HEADS * RWKV_HEAD
CHUNK = 64
LORA_PAD = 128
QHEAD_COLS = 2 * LANES
META_ROWS = 64
VMEM_LIMIT = 56 * 1024 * 1024


def _cparams(sem):
    return pltpu.CompilerParams(dimension_semantics=sem, vmem_limit_bytes=VMEM_LIMIT)


def _sigmoid(x):
    return 1.0 / (1.0 + jnp.exp(-x))


def _dot(a, b):
    return jnp.dot(a, b, preferred_element_type=F32)


def _dot_nt(a, b):
    return lax.dot_general(a, b, (((1,), (1,)), ((), ())), preferred_element_type=F32)


def _dot_tn(a, b):
    return lax.dot_general(a, b, (((0,), (0,)), ((), ())), preferred_element_type=F32)


def _rms(x, g):
    ms = jnp.mean(x * x, axis=-1, keepdims=True)
    return x * lax.rsqrt(ms + NORM_EPS) * g


def _ffn_kernel(x_hbm, g_ref, wg_ref, wu_ref, wd_ref, *rest, final_norm):
    if final_norm:
        g2_ref, o_ref, xbuf, xn_scr, sem = rest
    else:
        o_ref, xbuf, xn_scr, sem = rest
    i, j = pl.program_id(0), pl.program_id(1)
    n_i, n_j = pl.num_programs(0), pl.num_programs(1)
    tm = xbuf.shape[0]

    def x_copy(tile):
        rows = pl.ds(pl.multiple_of(tile * tm, tm), tm)
        return pltpu.make_async_copy(x_hbm.at[rows, :], xbuf, sem)

    @pl.when(jnp.logical_and(i == 0, j == 0))
    def _():
        x_copy(0).start()

    @pl.when(j == 0)
    def _():
        x_copy(i).wait()
        x = xbuf[...]
        xn_scr[...] = _rms(x, g_ref[...]).astype(BF16)
        o_ref[...] = 2.0 * x

    @pl.when(jnp.logical_and(j == 1, i + 1 < n_i))
    def _():
        x_copy(i + 1).start()

    xn = xn_scr[...]
    a = _dot(xn, wg_ref[...])
    b = _dot(xn, wu_ref[...])
    mid = (a * _sigmoid(a) * b).astype(BF16)
    o_ref[...] += _dot(mid, wd_ref[...])

    @pl.when(j == n_j - 1)
    def _():
        h = 0.5 * o_ref[...]
        o_ref[...] = _rms(h, g2_ref[...]) if final_norm else h


def _ffn(x, g, wg, wu, wd, g2=None):
    m, d = x.shape
    dff = wg.shape[1]
    tm = min(1024, m)
    tf = 512
    assert dff // tf >= 2
    row = pl.BlockSpec((tm, d), lambda i, j: (i, 0))
    vec = pl.BlockSpec((1, d), lambda i, j: (0, 0))
    args = [x, g, wg, wu, wd] + ([g2] if g2 is not None else [])
    return pl.pallas_call(
        functools.partial(_ffn_kernel, final_norm=g2 is not None),
        grid=(m // tm, dff // tf),
        in_specs=[pl.BlockSpec(memory_space=pl.ANY), vec,
                  pl.BlockSpec((d, tf), lambda i, j: (0, j)),
                  pl.BlockSpec((d, tf), lambda i, j: (0, j)),
                  pl.BlockSpec((tf, d), lambda i, j: (j, 0))] + ([vec] if g2 is not None else []),
        out_specs=row,
        out_shape=jax.ShapeDtypeStruct((m, d), F32),
        scratch_shapes=[pltpu.VMEM((tm, d), F32), pltpu.VMEM((tm, d), BF16),
                        pltpu.SemaphoreType.DMA(())],
        compiler_params=_cparams(("arbitrary", "arbitrary")),
        name="ffn",
    )(*args)


def _inproj_kernel(h_ref, g_ref, w_ref, o_ref, xn_scr):
    @pl.when(pl.program_id(1) == 0)
    def _():
        xn_scr[...] = _rms(h_ref[...], g_ref[...]).astype(BF16)

    o_ref[...] = _dot(xn_scr[...], w_ref[...]).astype(o_ref.dtype)


def _inproj(h, g, w, out_dtype):
    m, k = h.shape
    n = w.shape[1]
    tm = min(1024, m)
    tn = 2048 if n % 2048 == 0 else 1024
    return pl.pallas_call(
        _inproj_kernel,
        grid=(m // tm, n // tn),
        in_specs=[pl.BlockSpec((tm, k), lambda i, j: (i, 0)),
                  pl.BlockSpec((1, k), lambda i, j: (0, 0)),
                  pl.BlockSpec((k, tn), lambda i, j: (0, j))],
        out_specs=pl.BlockSpec((tm, tn), lambda i, j: (i, j)),
        out_shape=jax.ShapeDtypeStruct((m, n), out_dtype),
        scratch_shapes=[pltpu.VMEM((tm, k), BF16)],
        compiler_params=_cparams(("parallel", "arbitrary")),
        name="inproj",
    )(h, g, w)


def _iota(shape, dim):
    return lax.broadcasted_iota(jnp.int32, shape, dim)


def _split(x, parts):
    out = []
    for _ in range(parts - 1):
        hi = x.astype(BF16)
        out.append(hi)
        x = x - hi.astype(F32)
    out.append(x.astype(BF16))
    return out


def _dot01(m01, x, parts):
    return sum(_dot(m01, xp) for xp in _split(x, parts))


def _dot01_r(x, m01, parts):
    return sum(_dot(xp, m01) for xp in _split(x, parts))


def _wkv_masks(L, tt):
    C, H = GROUP_COLS, GROUP_HEADS
    hl = H * L
    r_hl, c_c = np.arange(hl)[:, None], np.arange(C)[None, :]
    t, s = np.arange(L)[:, None], np.arange(hl)[None, :] % L
    rt, ct = np.arange(tt)[:, None], np.arange(tt)[None, :]
    return dict(
        bd_lc=jnp.asarray(r_hl // L == c_c // RWKV_HEAD, BF16),
        bd_ll=jnp.asarray(r_hl // L == np.arange(hl)[None, :] // L, BF16),
        bd_cc=jnp.asarray(np.arange(C)[:, None] // RWKV_HEAD == c_c // RWKV_HEAD, F32),
        seg=jnp.asarray(np.arange(C)[:, None] // RWKV_HEAD == c_c // RWKV_HEAD, BF16),
        strict=jnp.asarray(t > s, F32),
        incl=jnp.asarray(t >= s, F32),
        eye=jnp.asarray(t == s, F32),
        tri=jnp.asarray((rt >= ct) & (rt // L == ct // L), BF16),
    )


_MASK_NAMES = ("bd_lc", "bd_ll", "bd_cc", "seg", "strict", "incl", "eye", "tri")
_TAIL_NAMES = ("wl", "bonus", "gate")
_SLOT1_NAMES = ("ar", "bk", "vbd", "abd", "bkh", "v") + _TAIL_NAMES
_SLOT2_NAMES = ("q", "y0", "g", "n") + _TAIL_NAMES


def _slot_shapes(tt, L):
    C, n_c, hl = GROUP_COLS, tt // L, GROUP_HEADS * L
    return dict(ar=((n_c, 2 * L, C), BF16), bk=((n_c, 2 * hl, C), BF16), vbd=((n_c, hl, C), BF16),
                abd=((n_c, hl, C), BF16), bkh=((n_c, 2 * L, C), BF16), v=((tt, C), BF16),
                wl=((n_c, 1, C), F32), bonus=((tt, C), F32), gate=((tt, C), F32),
                q=((n_c, L, C), BF16), y0=((n_c, L, C), F32), g=((n_c, C, C), BF16),
                n=((n_c, C, C), F32))


def _shift_rows(x, carry_row):
    prev = pltpu.roll(x, 1, axis=0)
    return jnp.where(_iota(x.shape, 0) == 0, carry_row, prev)


def _bf(x):
    return x.astype(BF16)


def _cat(axis, *xs):
    return jnp.concatenate(xs, axis=axis)


def _tile_rows(x):
    return jnp.concatenate([x] * GROUP_HEADS, axis=0)


def _each(f, *lists):
    return [f(*xs) for xs in zip(*lists)]


def _rwkv_prep(tp, gp, n_g, L, mk, refs, wr):
    (pr_ref, pk_ref, pv_ref, plo_ref, cr_ref, ck_ref, cv_ref, clo_ref,
     mur_ref, muk_ref, muv_ref, mulo_ref, w0_ref, a0_ref, kk_ref, ka_ref, rk_ref,
     wup_ref, aup_ref, gup_ref, crkv_scr, clo_scr) = refs
    tt = pr_ref.shape[0]
    first = tp == 0

    def shifted(p_ref, init_ref, carry16, mu):
        x = p_ref[...].astype(F32)
        carry = jnp.where(first, init_ref[...], carry16)
        prev = _shift_rows(x, carry[15:16, :].astype(F32))
        return x + mu * (prev - x)

    r = shifted(pr_ref, cr_ref, crkv_scr[3 * gp + 0], mur_ref[...])
    k = shifted(pk_ref, ck_ref, crkv_scr[3 * gp + 1], muk_ref[...])
    v = shifted(pv_ref, cv_ref, crkv_scr[3 * gp + 2], muv_ref[...])
    lo_carry = clo_scr[...]
    lo = shifted(plo_ref, clo_ref, lo_carry, mulo_ref[...])
    crkv_scr[3 * gp + 0] = pr_ref[tt - 16:tt, :]
    crkv_scr[3 * gp + 1] = pk_ref[tt - 16:tt, :]
    crkv_scr[3 * gp + 2] = pv_ref[tt - 16:tt, :]
    clo_scr[...] = jnp.where(gp == n_g - 1, plo_ref[tt - 16:tt, :], lo_carry)
    xw = lo[:, 0:LORA_PAD]
    xa = lo[:, LORA_PAD:2 * LORA_PAD]
    xg = lo[:, 2 * LORA_PAD:]
    kkr = k * kk_ref[...]
    yield

    z = -(w0_ref[...] + _dot(_bf(jnp.tanh(xw)), wup_ref[...]))
    alpha = _sigmoid(a0_ref[...] + _dot(_bf(xa), aup_ref[...]))
    wr["gate"][...] = _dot(_bf(_sigmoid(xg)), gup_ref[...])
    kk = kkr * lax.rsqrt(jnp.maximum(_dot01_r(kkr * kkr, mk["seg"][...], 2), 1e-24))
    softplus = jnp.maximum(z, 0.0) + jnp.log1p(jnp.exp(-jnp.abs(z)))
    lw = -jnp.exp(-softplus - 0.5)
    k2 = k * (1.0 + (alpha - 1.0) * ka_ref[...])
    b = kk * alpha
    yield

    cum = _dot01(mk["tri"][...], lw, 3)
    wr["bonus"][...] = _dot01_r(r * k2 * rk_ref[...], mk["seg"][...], 2) * v
    e_n = jnp.exp(-cum)
    a_t = _bf(-kk * jnp.exp(cum - lw))
    r_t = _bf(r * jnp.exp(cum))
    b_t = _bf(b * e_n)
    k_t = _bf(k2 * e_n)
    vb = _bf(v)
    wr["v"][...] = vb
    bd_lc = mk["bd_lc"][...]
    for c in range(tt // L):
        sl = slice(c * L, (c + 1) * L)
        cl = cum[sl.stop - 1:sl.stop, :]
        e_l = jnp.exp(cl - cum[sl])
        wr["wl"][c] = jnp.exp(cl)
        wr["ar"][c, :L] = a_t[sl]
        wr["ar"][c, L:] = r_t[sl]
        wr["bk"][c, :GROUP_HEADS * L] = _tile_rows(b_t[sl]) * bd_lc
        wr["bk"][c, GROUP_HEADS * L:] = _tile_rows(k_t[sl]) * bd_lc
        wr["vbd"][c] = _tile_rows(vb[sl]) * bd_lc
        wr["abd"][c] = _tile_rows(a_t[sl]) * bd_lc
        wr["bkh"][c, :L] = _bf(b[sl] * e_l)
        wr["bkh"][c, L:] = _bf(k2[sl] * e_l)


def _rwkv_local(L, mk, rd, wr):
    tt = rd["v"].shape[0]
    n_c = tt // L
    n_h = GROUP_HEADS * L
    C = GROUP_COLS
    chunks = [slice(c * L, (c + 1) * L) for c in range(n_c)]
    bd_lc, bd_ll = mk["bd_lc"][...], mk["bd_ll"][...]
    strict, incl = mk["strict"][...] > 0, mk["incl"][...] > 0
    bd_cc = mk["bd_cc"][...] > 0
    bd = lambda xb: _tile_rows(xb) * bd_lc
    bd_sq = lambda pb: _tile_rows(pb) * bd_ll
    for k in _TAIL_NAMES:
        wr[k][...] = rd[k][...]

    ar = [rd["ar"][c] for c in range(n_c)]
    aa = _each(_dot_nt, ar, [rd["bk"][c] for c in range(n_c)])
    yield
    a_ab = _each(lambda x: jnp.where(strict, x[:L, :n_h], 0.0), aa)
    a_ak = _each(lambda x: _bf(jnp.where(strict, x[:L, n_h:], 0.0)), aa)
    a_rb = _each(lambda x: _bf(jnp.where(incl, x[L:, :n_h], 0.0)), aa)
    a_rk = _each(lambda x: _bf(jnp.where(incl, x[L:, n_h:], 0.0)), aa)

    n_steps = int(np.log2(L))
    eye = mk["eye"][...]
    tp = _each(lambda x: eye + x, a_ab)
    p = _each(lambda x: _dot(_bf(x), bd_sq(_bf(x))), a_ab)
    yield
    for _ in range(1, n_steps - 1):
        out = _each(lambda x, t: _dot(_bf(_cat(0, x, t)), bd_sq(_bf(x))), p, tp)
        yield
        p = _each(lambda o: o[:L], out)
        tp = _each(lambda t, o: t + o[L:], tp, out)
    tb = _each(lambda t, x: _bf(t + _dot(_bf(t), bd_sq(_bf(x)))), tp, p)
    yield

    v_bd = [rd["vbd"][c] for c in range(n_c)]
    x0 = _each(_dot, a_ak, v_bd)
    yield
    ut = _each(lambda t, x, c: _dot(t, _cat(1, bd(_bf(x)), rd["abd"][c])), tb, x0, range(n_c))
    yield
    u0 = _each(lambda x: _bf(x[:, :C]), ut)
    ta = _each(lambda x: _bf(x[:, C:]), ut)
    for c in range(n_c):
        wr["q"][c] = _bf(ar[c][L:].astype(F32) + _dot(a_rb[c], bd(ta[c])))
    yield
    for c in range(n_c):
        wr["y0"][c] = _dot(_cat(1, a_rb[c], a_rk[c]), _cat(0, bd(u0[c]), v_bd[c]))
    yield
    for c in range(n_c):
        bkh = rd["bkh"][c]
        wr["g"][c] = _bf(jnp.where(bd_cc, _dot_tn(ta[c], bkh[:L]), 0.0))
        wr["n"][c] = jnp.where(bd_cc, _dot_tn(_cat(0, u0[c], rd["v"][chunks[c]]), bkh), 0.0)


def _rwkv_serial(tc, gc, L, mk, refs, rd, emit_state):
    s0_ref, gnw_ref, gnb_ref, y_ref, sout_ref, s_scr = refs
    n_c = rd["q"].shape[0]
    S = jnp.where(tc == 0, s0_ref[gc], s_scr[gc])
    ys = []
    for c in range(n_c):
        sb = _bf(S)
        ys.append(rd["y0"][c] + _dot_nt(rd["q"][c], sb))
        S = S * rd["wl"][c] + _dot(sb, rd["g"][c]) + rd["n"][c]
        yield
    s_scr[gc] = S
    if emit_state:
        sout_ref[0] = S
    y = ys[0] if n_c == 1 else _cat(0, *ys)

    inv_n = 1.0 / RWKV_HEAD
    seg = mk["seg"][...]
    mean = _dot01_r(y, seg, 2) * inv_n
    yield
    d = y - mean
    var = _dot01_r(d * d, seg, 2) * inv_n
    yield
    yn = d * lax.rsqrt(var + GN_EPS) * gnw_ref[...] + gnb_ref[...]
    y_ref[...] = ((yn + rd["bonus"][...]) * rd["gate"][...]).astype(y_ref.dtype)


def _weave(gens, order):
    for who in order:
        next(gens[who], None)
    for gen in gens.values():
        for _ in gen:
            pass


_WEAVE = "ppp" + "ls" * 12


def _rwkv_kernel(*refs, L, n_t, n_g, emit_state):
    n_in = 20 + len(_MASK_NAMES) + 3
    in_refs, rest = refs[:n_in], refs[n_in:]
    prep_in, masks, chain_in = in_refs[:20], in_refs[20:20 + len(_MASK_NAMES)], in_refs[-3:]
    mk = dict(zip(_MASK_NAMES, masks))
    if emit_state:
        y_ref, sout_ref = rest[:2]
        scr = rest[2:]
    else:
        y_ref, sout_ref = rest[0], None
        scr = rest[1:]
    s_scr, crkv_scr, clo_scr = scr[:3]
    scr = list(scr[3:])
    slots1 = [dict(zip(_SLOT1_NAMES, [scr.pop(0) for _ in _SLOT1_NAMES])) for _ in range(2)]
    slots2 = [dict(zip(_SLOT2_NAMES, [scr.pop(0) for _ in _SLOT2_NAMES])) for _ in range(2)]

    s = pl.program_id(0)
    n = pl.num_programs(0) - 2
    sp = jnp.minimum(s, n - 1)
    sc = jnp.clip(s - 2, 0, n - 1)
    tp, gp = (sp // n_g) % n_t, sp % n_g
    tc, gc = (sc // n_g) % n_t, sc % n_g

    @pl.when(s == 0)
    def _():
        for ref in (tuple(slots1[1].values()) + tuple(slots2[0].values()) + tuple(slots2[1].values())
                    + (s_scr, crkv_scr, clo_scr)):
            ref[...] = jnp.zeros_like(ref)

    def step(parity):
        gens = dict(
            p=_rwkv_prep(tp, gp, n_g, L, mk, prep_in + (crkv_scr, clo_scr), slots1[parity]),
            l=_rwkv_local(L, mk, slots1[1 - parity], slots2[1 - parity]),
            s=_rwkv_serial(tc, gc, L, mk, chain_in + (y_ref, sout_ref, s_scr), slots2[parity],
                           emit_state))
        _weave(gens, _WEAVE)

    for parity in range(2):
        pl.when(s % 2 == parity)(functools.partial(step, parity))


def _rwkv(proj, carry16, s0, rp, *, bsz, seq, d, off_rkv, off_lo, emit_state):
    C = GROUP_COLS
    n_g = d // C
    tt = min(512, seq)
    L = min(CHUNK, tt)
    n_t = seq // tt
    assert n_t == 1 or not emit_state
    n = bsz * n_t * n_g
    lo_w = 4 * LORA_PAD
    cb = off_rkv // C
    lb = off_lo // lo_w
    masks = _wkv_masks(L, tt)

    def tile(s):
        return s // n_g, s % n_g

    def prep_ix(f):
        return lambda s: f(*tile(jnp.minimum(s, n - 1)))

    def serial_ix(f):
        return lambda s: f(*tile(jnp.clip(s - 2, 0, n - 1)))

    pblk = lambda j: pl.BlockSpec((tt, C), prep_ix(lambda r, g: (r, cb + j * n_g + g)))
    cblk = lambda j: pl.BlockSpec((16, C), prep_ix(lambda r, g: (0, cb + j * n_g + g)))
    vblk = lambda j=0: pl.BlockSpec((1, C), prep_ix(lambda r, g: (0, j * n_g + g)))
    const = lambda a: pl.BlockSpec(a.shape, lambda s, nd=a.ndim: (0,) * nd)
    in_specs = [
        pblk(0), pblk(1), pblk(2),
        pl.BlockSpec((tt, lo_w), prep_ix(lambda r, g: (r, lb))),
        cblk(0), cblk(1), cblk(2),
        pl.BlockSpec((16, lo_w), lambda s: (0, lb)),
        vblk(0), vblk(1), vblk(2),
        pl.BlockSpec((1, lo_w), lambda s: (0, 0)),
        vblk(), vblk(), vblk(), vblk(), vblk(),
        pl.BlockSpec((LORA_PAD, C), prep_ix(lambda r, g: (0, g))),
        pl.BlockSpec((LORA_PAD, C), prep_ix(lambda r, g: (0, g))),
        pl.BlockSpec((2 * LORA_PAD, C), prep_ix(lambda r, g: (0, g))),
    ] + [const(masks[k]) for k in _MASK_NAMES] + [
        const(s0),
        pl.BlockSpec((1, C), serial_ix(lambda r, g: (0, g))),
        pl.BlockSpec((1, C), serial_ix(lambda r, g: (0, g))),
    ]
    out_specs = [pl.BlockSpec((tt, C), serial_ix(lambda r, g: (r, g)))]
    out_shape = [jax.ShapeDtypeStruct((bsz * seq, d), BF16)]
    if emit_state:
        out_specs.append(pl.BlockSpec((1, C, C), serial_ix(lambda r, g: (r * n_g + g, 0, 0))))
        out_shape.append(jax.ShapeDtypeStruct((bsz * n_g, C, C), F32))
    shp = _slot_shapes(tt, L)
    scratch = [pltpu.VMEM((n_g, C, C), F32), pltpu.VMEM((3 * n_g, 16, C), BF16),
               pltpu.VMEM((16, lo_w), BF16)]
    scratch += [pltpu.VMEM(*shp[k]) for _ in range(2) for k in _SLOT1_NAMES]
    scratch += [pltpu.VMEM(*shp[k]) for _ in range(2) for k in _SLOT2_NAMES]
    return pl.pallas_call(
        functools.partial(_rwkv_kernel, L=L, n_t=n_t, n_g=n_g, emit_state=emit_state),
        grid=(n + 2,),
        in_specs=in_specs,
        out_specs=out_specs,
        out_shape=out_shape,
        scratch_shapes=scratch,
        compiler_params=_cparams(("arbitrary",)),
        name="rwkv",
    )(proj, proj, proj, proj, carry16, carry16, carry16, carry16,
      rp["mu_rkv"], rp["mu_rkv"], rp["mu_rkv"], rp["mu_lo"],
      rp["w0"], rp["a0"], rp["k_k"], rp["k_a"], rp["r_k"],
      rp["w_up"], rp["a_up"], rp["g_up"],
      *[masks[k] for k in _MASK_NAMES], s0, rp["gn_w"], rp["gn_b"])


def _rope(x, c, s1, s2):
    return x * c + pltpu.roll(x, 96, axis=1) * s1 + pltpu.roll(x, 32, axis=1) * s2


def _mla_prep_kernel(cq_ref, ckv_ref, kpe_ref, qn_ref, kvn_ref, wq_ref, wkv_ref,
                     c_ref, s1_ref, s2_ref, q_ref, kn_ref, v_ref, kr_ref, *, n_heads):
    c, s1, s2 = c_ref[...], s1_ref[...], s2_ref[...]
    scale = QK_DIM ** -0.5 * np.log2(np.e)
    cq = _rms(cq_ref[...].astype(F32), qn_ref[...]).astype(BF16)
    for h in range(n_heads):
        qh = _dot(cq, wq_ref[:, h * QHEAD_COLS:(h + 1) * QHEAD_COLS]) * scale
        q_ref[:, h * QHEAD_COLS:h * QHEAD_COLS + LANES] = qh[:, :LANES].astype(BF16)
        q_ref[:, h * QHEAD_COLS + LANES:(h + 1) * QHEAD_COLS] = (
            _rope(qh[:, LANES:], c, s1, s2).astype(BF16))
    ckv = _rms(ckv_ref[...].astype(F32), kvn_ref[...]).astype(BF16)
    hw = n_heads * NOPE_DIM
    kn_ref[...] = _dot(ckv, wkv_ref[:, :hw]).astype(BF16)
    v_ref[...] = _dot(ckv, wkv_ref[:, hw:]).astype(BF16)
    kr_ref[...] = _rope(kpe_ref[...].astype(F32), c, s1, s2).astype(BF16)


def _mla_prep(proj, mp, tabs, *, seq, n_heads, off_cq, off_ckv, off_kpe):
    m = proj.shape[0]
    tm = min(512, seq)
    n_t = seq // tm
    ql = mp["w_uq"].shape[0]
    kl = mp["w_ukv"].shape[0]
    hw = n_heads * NOPE_DIM
    tab = pl.BlockSpec((tm, LANES), lambda i: (i % n_t, 0))
    full = lambda a: pl.BlockSpec(a.shape, lambda i: (0, 0))
    return pl.pallas_call(
        functools.partial(_mla_prep_kernel, n_heads=n_heads),
        grid=(m // tm,),
        in_specs=[pl.BlockSpec((tm, ql), lambda i: (i, off_cq // ql)),
                  pl.BlockSpec((tm, kl), lambda i: (i, off_ckv // kl)),
                  pl.BlockSpec((tm, LANES), lambda i: (i, off_kpe // LANES)),
                  full(mp["q_norm"]), full(mp["kv_norm"]), full(mp["w_uq"]), full(mp["w_ukv"]),
                  tab, tab, tab],
        out_specs=[pl.BlockSpec((tm, n_heads * QHEAD_COLS), lambda i: (i, 0)),
                   pl.BlockSpec((tm, hw), lambda i: (i, 0)),
                   pl.BlockSpec((tm, hw), lambda i: (i, 0)),
                   pl.BlockSpec((tm, LANES), lambda i: (i, 0))],
        out_shape=[jax.ShapeDtypeStruct((m, n_heads * QHEAD_COLS), BF16),
                   jax.ShapeDtypeStruct((m, hw), BF16),
                   jax.ShapeDtypeStruct((m, hw), BF16),
                   jax.ShapeDtypeStruct((m, LANES), BF16)],
        compiler_params=_cparams(("parallel",)),
        name="mla_prep",
    )(proj, proj, proj, mp["q_norm"], mp["kv_norm"], mp["w_uq"], mp["w_ukv"], *tabs)


def _attn_kernel(q_ref, kn_ref, kr_ref, v_ref, km_ref, vm_ref, o_ref, *, tq):
    seq = q_ref.shape[0]
    n_q = seq // tq
    qs = [q_ref[i * tq:(i + 1) * tq, :] for i in range(n_q)]
    kts = [jnp.concatenate([kn_ref[j * tq:(j + 1) * tq, :], kr_ref[j * tq:(j + 1) * tq, :]], axis=1)
           for j in range(n_q)]
    vts = [v_ref[j * tq:(j + 1) * tq, :] for j in range(n_q)]
    causal = _iota((tq, tq), 1) <= _iota((tq, tq), 0)

    ss = [_dot_nt(q, km_ref[0]) for q in qs]
    ms = [jnp.max(s, axis=-1, keepdims=True) for s in ss]
    ps = [jnp.exp2(s - m) for s, m in zip(ss, ms)]
    ls = [jnp.sum(p, axis=-1, keepdims=True) for p in ps]
    accs = [_dot(p.astype(BF16), vm_ref[0]) for p in ps]

    for d in range(n_q):
        act = list(range(d, n_q))
        ss = [_dot_nt(qs[i], kts[i - d]) for i in act]
        if d == 0:
            ss = [jnp.where(causal, s, -1e30) for s in ss]
        m_new = [jnp.maximum(ms[i], jnp.max(s, axis=-1, keepdims=True)) for i, s in zip(act, ss)]
        corr = [jnp.exp2(ms[i] - mn) for i, mn in zip(act, m_new)]
        ps = [jnp.exp2(s - mn) for s, mn in zip(ss, m_new)]
        pv = [_dot(p.astype(BF16), vts[i - d]) for i, p in zip(act, ps)]
        for n, i in enumerate(act):
            ms[i] = m_new[n]
            ls[i] = corr[n] * ls[i] + jnp.sum(ps[n], axis=-1, keepdims=True)
            accs[i] = corr[n] * accs[i] + pv[n]
    for i in range(n_q):
        o_ref[i * tq:(i + 1) * tq, :] = (accs[i] / ls[i]).astype(o_ref.dtype)


def _attn(q, kn, kr, v, k_meta, v_meta, *, bsz, seq, n_heads):
    tq = min(512, seq)
    return pl.pallas_call(
        functools.partial(_attn_kernel, tq=tq),
        grid=(bsz, n_heads),
        in_specs=[pl.BlockSpec((seq, QHEAD_COLS), lambda b, h: (b, h)),
                  pl.BlockSpec((seq, NOPE_DIM), lambda b, h: (b, h)),
                  pl.BlockSpec((seq, LANES), lambda b, h: (b, 0)),
                  pl.BlockSpec((seq, V_DIM), lambda b, h: (b, h)),
                  pl.BlockSpec((1, N_META, QHEAD_COLS), lambda b, h: (h, 0, 0)),
                  pl.BlockSpec((1, N_META, V_DIM), lambda b, h: (h, 0, 0))],
        out_specs=pl.BlockSpec((seq, V_DIM), lambda b, h: (b, h)),
        out_shape=jax.ShapeDtypeStruct((bsz * seq, n_heads * V_DIM), BF16),
        compiler_params=_cparams(("parallel", "parallel")),
        name="attn",
    )(q, kn, kr, v, k_meta, v_meta)


def _outproj_kernel(h_ref, ya_ref, yb_ref, ga_ref, gb_ref, w_ref, o_ref):
    mix = (_sigmoid(ga_ref[...].astype(F32)) * ya_ref[...].astype(F32)
           + _sigmoid(gb_ref[...].astype(F32)) * yb_ref[...].astype(F32))
    o_ref[...] = h_ref[...] + _dot(mix.astype(BF16), w_ref[...])


def _outproj(h, ya, yb, proj, w_out, *, off_gate):
    m, d = h.shape
    tm = min(512, m)
    gb = off_gate // d
    row = pl.BlockSpec((tm, d), lambda i: (i, 0))
    return pl.pallas_call(
        _outproj_kernel,
        grid=(m // tm,),
        in_specs=[row, row, row,
                  pl.BlockSpec((tm, d), lambda i: (i, gb)),
                  pl.BlockSpec((tm, d), lambda i: (i, gb + 1)),
                  pl.BlockSpec((d, d), lambda i: (0, 0))],
        out_specs=row,
        out_shape=jax.ShapeDtypeStruct((m, d), F32),
        compiler_params=_cparams(("parallel",)),
        name="outproj",
    )(h, ya, yb, proj, proj, w_out)


def _pad_cols(w, n):
    return jnp.pad(w, ((0, 0), (0, n - w.shape[1])))


def _pad_rows(w, n):
    return jnp.pad(w, ((0, n - w.shape[0]), (0, 0)))


def _rope_tables(pos):
    inv_freq = 1.0 / (ROPE_THETA ** (jnp.arange(0, ROPE_DIM, 2, dtype=F32) / ROPE_DIM))
    ang = pos.astype(F32)[:, None] * inv_freq[None, :]
    cos, sin = jnp.cos(ang), jnp.sin(ang)
    z = jnp.zeros_like(cos)
    return (jnp.concatenate([cos, cos, z, z], axis=1),
            jnp.concatenate([-sin, z, z, z], axis=1),
            jnp.concatenate([z, sin, z, z], axis=1))


def kernel(x, meta_tokens, ffn1_norm, ffn1_w_gate, ffn1_w_up, ffn1_w_down, mix_norm, w_in,
           tm_mu, w0, w_up, a0, a_up, g_up, k_k, k_a, r_k, gn_w, gn_b, q_norm, w_uq,
           kv_norm, w_ukv, w_out, ffn2_norm, ffn2_w_gate, ffn2_w_up, ffn2_w_down, final_norm):
    bsz, seq, d = x.shape
    assert d % GROUP_COLS == 0 and seq % 16 == 0
    assert ffn1_norm.shape[0] == 1, "single-layer stack"
    n_mla = d // 128
    w_lora, a_lora, g_lora = w_up.shape[1], a_up.shape[1], g_up.shape[1]
    q_lora, kv_lora = w_uq.shape[1], w_ukv.shape[1]
    assert w_lora <= LORA_PAD and a_lora <= LORA_PAD and g_lora == 2 * LORA_PAD
    rwkv_cols = 3 * d + w_lora + a_lora + g_lora
    mla_cols = q_lora + kv_lora + ROPE_DIM

    wi = w_in[0]
    o1, o2 = 3 * d, 3 * d + w_lora
    o3 = o2 + a_lora
    wi_rkv = wi[:, :3 * d]
    wi_lo = jnp.concatenate([_pad_cols(wi[:, o1:o2], LORA_PAD), _pad_cols(wi[:, o2:o3], LORA_PAD),
                             wi[:, o3:rwkv_cols]], axis=1)
    m0 = rwkv_cols
    wi_cq = wi[:, m0:m0 + q_lora]
    wi_ckv = wi[:, m0 + q_lora:m0 + q_lora + kv_lora]
    wi_kpe = _pad_cols(wi[:, m0 + q_lora + kv_lora:m0 + mla_cols], LANES)
    wi_gate = wi[:, m0 + mla_cols:]
    parts = [wi_gate, wi_rkv, wi_lo, wi_cq, wi_ckv, wi_kpe]
    offs = np.cumsum([0] + [p.shape[1] for p in parts])
    off_gate, off_rkv, off_lo, off_cq, off_ckv, off_kpe, n_used = (int(o) for o in offs)
    n_proj = -(-n_used // 1024) * 1024
    parts.append(jnp.zeros((d, n_proj - n_used), wi.dtype))
    w_proj = jnp.concatenate([p.astype(BF16) for p in parts], axis=1)

    mu = tm_mu[0]
    rp = dict(
        mu_rkv=mu[None, :3 * d],
        mu_lo=jnp.concatenate([jnp.pad(mu[o1:o2], (0, LORA_PAD - w_lora)),
                               jnp.pad(mu[o2:o3], (0, LORA_PAD - a_lora)), mu[o3:]])[None, :],
        w0=w0, a0=a0, k_k=k_k, k_a=k_a, r_k=r_k[0].reshape(1, d), gn_w=gn_w, gn_b=gn_b,
        w_up=_pad_rows(w_up[0], LORA_PAD).astype(BF16),
        a_up=_pad_rows(a_up[0], LORA_PAD).astype(BF16),
        g_up=g_up[0].astype(BF16),
    )

    wq = w_uq[0].reshape(q_lora, n_mla, QK_DIM)
    wq = jnp.pad(wq, ((0, 0), (0, 0), (0, QHEAD_COLS - QK_DIM))).reshape(q_lora, n_mla * QHEAD_COLS)
    wkv = w_ukv[0].reshape(kv_lora, n_mla, NOPE_DIM + V_DIM)
    wkv = jnp.concatenate([wkv[:, :, :NOPE_DIM].reshape(kv_lora, -1),
                           wkv[:, :, NOPE_DIM:].reshape(kv_lora, -1)], axis=1)
    mp = dict(q_norm=q_norm, kv_norm=kv_norm, w_uq=wq.astype(BF16), w_ukv=wkv.astype(BF16))

    f1 = (ffn1_w_gate[0].astype(BF16), ffn1_w_up[0].astype(BF16), ffn1_w_down[0].astype(BF16))
    f2 = (ffn2_w_gate[0].astype(BF16), ffn2_w_up[0].astype(BF16), ffn2_w_down[0].astype(BF16))
    w_o = w_out[0].astype(BF16)

    def front(h_rows, *, b, s, pos, carry16, s0, emit_state):
        h1 = _ffn(h_rows, ffn1_norm, *f1)
        proj = _inproj(h1, mix_norm, w_proj, BF16)
        rw = _rwkv(proj, carry16, s0, rp, bsz=b, seq=s, d=d, off_rkv=off_rkv, off_lo=off_lo,
                   emit_state=emit_state)
        q, kn, v, kr = _mla_prep(proj, mp, _rope_tables(pos), seq=s, n_heads=n_mla,
                                 off_cq=off_cq, off_ckv=off_ckv, off_kpe=off_kpe)
        return h1, proj, rw, (q, kn, kr, v)

    n_g = d // GROUP_COLS
    pad = META_ROWS - N_META
    meta_h = jnp.concatenate([jnp.zeros((pad, d), F32), meta_tokens.astype(F32)], axis=0)
    meta_pos = jnp.maximum(jnp.arange(META_ROWS) - pad, 0)
    _, proj_m, (_, s_meta), (_, kn_m, kr_m, v_m) = front(
        meta_h, b=1, s=META_ROWS, pos=meta_pos,
        carry16=jnp.zeros((16, n_proj), BF16), s0=jnp.zeros((n_g, GROUP_COLS, GROUP_COLS), F32),
        emit_state=True)
    kn_m = kn_m[pad:].reshape(N_META, n_mla, NOPE_DIM).transpose(1, 0, 2)
    kr_m = jnp.broadcast_to(kr_m[pad:][None], (n_mla, N_META, LANES))
    k_meta = jnp.concatenate([kn_m, kr_m], axis=2)
    v_meta = v_m[pad:].reshape(N_META, n_mla, V_DIM).transpose(1, 0, 2)

    h0 = x.reshape(bsz * seq, d)
    h1, proj, (ya,), (q, kn, kr, v) = front(
        h0, b=bsz, s=seq, pos=N_META + jnp.arange(seq),
        carry16=proj_m[META_ROWS - 16:], s0=s_meta, emit_state=False)
    yb = _attn(q, kn, kr, v, k_meta, v_meta, bsz=bsz, seq=seq, n_heads=n_mla)
    h2 = _outproj(h1, ya, yb, proj, w_o, off_gate=off_gate)
    y = _ffn(h2, ffn2_norm, *f2, final_norm[None, :])
    return y.reshape(bsz, seq, d)
```

```python
import functools

import numpy as np
import jax
import jax.numpy as jnp
from jax import lax
from jax.experimental import pallas as pl
from jax.experimental.pallas import tpu as pltpu

F32 = jnp.float32
BF16 = jnp.bfloat16

N_META = 16
NORM_EPS = 1e-6
RWKV_HEAD = 64
GN_EPS = RWKV_HEAD * 1e-5
NOPE_DIM = 128
ROPE_DIM = 64
V_DIM = 128
QK_DIM = NOPE_DIM + ROPE_DIM
ROPE_THETA = 10000.0

LANES = 128
GROUP_HEADS = 4
GROUP_COLS = GROUP_HEADS * RWKV_HEAD
CHUNK = 64
RWKV_TILE = 512
LORA_PAD = 128
QHEAD_COLS = 2 * LANES
META_ROWS = 64
VMEM_LIMIT = 56 * 1024 * 1024


def _cparams(sem):
    return pltpu.CompilerParams(dimension_semantics=sem, vmem_limit_bytes=VMEM_LIMIT)


def _sigmoid(x):
    return 1.0 / (1.0 + jnp.exp(-x))


def _dot(a, b):
    return jnp.dot(a, b, preferred_element_type=F32)


def _dot_nt(a, b):
    return lax.dot_general(a, b, (((1,), (1,)), ((), ())), preferred_element_type=F32)


def _dot_tn(a, b):
    return lax.dot_general(a, b, (((0,), (0,)), ((), ())), preferred_element_type=F32)


def _rms(x, g):
    ms = jnp.mean(x * x, axis=-1, keepdims=True)
    return x * lax.rsqrt(ms + NORM_EPS) * g


def _cast_kernel(w_ref, o_ref):
    o_ref[...] = w_ref[...].astype(o_ref.dtype)


def _to_bf16(w):
    k, n = w.shape
    tk = min(256, k)
    blk = pl.BlockSpec((tk, n), lambda i: (i, 0))
    return pl.pallas_call(
        _cast_kernel,
        grid=(k // tk,),
        in_specs=[blk],
        out_specs=blk,
        out_shape=jax.ShapeDtypeStruct((k, n), BF16),
        compiler_params=_cparams(("parallel",)),
        name="cast",
    )(w)


def _ffn_kernel(x_hbm, g_ref, wg_ref, wu_ref, wd_ref, *rest, final_norm):
    if final_norm:
        g2_ref, o_ref, xbuf, xn_scr, sem = rest
    else:
        o_ref, xbuf, xn_scr, sem = rest
    i, j = pl.program_id(0), pl.program_id(1)
    n_i, n_j = pl.num_programs(0), pl.num_programs(1)
    tm = xbuf.shape[0]

    def x_copy(tile):
        rows = pl.ds(pl.multiple_of(tile * tm, tm), tm)
        return pltpu.make_async_copy(x_hbm.at[rows, :], xbuf, sem)

    @pl.when(jnp.logical_and(i == 0, j == 0))
    def _():
        x_copy(0).start()

    @pl.when(j == 0)
    def _():
        x_copy(i).wait()
        x = xbuf[...]
        xn_scr[...] = _rms(x, g_ref[...]).astype(BF16)
        o_ref[...] = 2.0 * x

    @pl.when(jnp.logical_and(j == 1, i + 1 < n_i))
    def _():
        x_copy(i + 1).start()

    xn = xn_scr[...]
    a = _dot(xn, wg_ref[...])
    b = _dot(xn, wu_ref[...])
    mid = (a * _sigmoid(a) * b).astype(BF16)
    o_ref[...] += _dot(mid, wd_ref[...])

    @pl.when(j == n_j - 1)
    def _():
        h = 0.5 * o_ref[...]
        o_ref[...] = _rms(h, g2_ref[...]) if final_norm else h


def _ffn(x, g, wg, wu, wd, g2=None):
    m, d = x.shape
    dff = wg.shape[1]
    tm = min(1024, m)
    tf = 512
    assert dff // tf >= 2
    row = pl.BlockSpec((tm, d), lambda i, j: (i, 0))
    vec = pl.BlockSpec((1, d), lambda i, j: (0, 0))
    args = [x, g, wg, wu, wd] + ([g2] if g2 is not None else [])
    return pl.pallas_call(
        functools.partial(_ffn_kernel, final_norm=g2 is not None),
        grid=(m // tm, dff // tf),
        in_specs=[pl.BlockSpec(memory_space=pl.ANY), vec,
                  pl.BlockSpec((d, tf), lambda i, j: (0, j)),
                  pl.BlockSpec((d, tf), lambda i, j: (0, j)),
                  pl.BlockSpec((tf, d), lambda i, j: (j, 0))] + ([vec] if g2 is not None else []),
        out_specs=row,
        out_shape=jax.ShapeDtypeStruct((m, d), F32),
        scratch_shapes=[pltpu.VMEM((tm, d), F32), pltpu.VMEM((tm, d), BF16),
                        pltpu.SemaphoreType.DMA(())],
        compiler_params=_cparams(("arbitrary", "arbitrary")),
        name="ffn",
    )(*args)


def _inproj_kernel(h_ref, g_ref, w_ref, o_ref, xn_scr):
    @pl.when(pl.program_id(1) == 0)
    def _():
        xn_scr[...] = _rms(h_ref[...], g_ref[...]).astype(BF16)

    o_ref[...] = _dot(xn_scr[...], w_ref[...]).astype(o_ref.dtype)


def _inproj(h, g, w, out_dtype):
    m, k = h.shape
    n = w.shape[1]
    tm = min(1024, m)
    tn = 2048 if n % 2048 == 0 else 1024
    return pl.pallas_call(
        _inproj_kernel,
        grid=(m // tm, n // tn),
        in_specs=[pl.BlockSpec((tm, k), lambda i, j: (i, 0)),
                  pl.BlockSpec((1, k), lambda i, j: (0, 0)),
                  pl.BlockSpec((k, tn), lambda i, j: (0, j))],
        out_specs=pl.BlockSpec((tm, tn), lambda i, j: (i, j)),
        out_shape=jax.ShapeDtypeStruct((m, n), out_dtype),
        scratch_shapes=[pltpu.VMEM((tm, k), BF16)],
        compiler_params=_cparams(("parallel", "arbitrary")),
        name="inproj",
    )(h, g, w)


def _iota(shape, dim):
    return lax.broadcasted_iota(jnp.int32, shape, dim)


def _split(x, parts):
    out = []
    for _ in range(parts - 1):
        hi = x.astype(BF16)
        out.append(hi)
        x = x - hi.astype(F32)
    out.append(x.astype(BF16))
    return out


def _dot01(m01, x, parts):
    return sum(_dot(m01, xp) for xp in _split(x, parts))


def _wkv_masks(L):
    C, H = GROUP_COLS, GROUP_HEADS
    hl = H * L
    r_hl, c_c = np.arange(hl)[:, None], np.arange(C)[None, :]
    t, s = np.arange(L)[:, None], np.arange(hl)[None, :] % L
    assert hl == C
    seg = np.arange(C)[:, None] // RWKV_HEAD == c_c // RWKV_HEAD
    sq = np.stack([r_hl // L == c_c // RWKV_HEAD, r_hl // L == np.arange(hl)[None, :] // L, seg])
    return (jnp.asarray(sq, BF16),
            jnp.asarray(seg, F32),
            jnp.asarray(np.stack([t > s, t >= s, t == s]), F32),
            jnp.asarray(np.arange(L)[:, None] >= np.arange(L)[None, :], BF16))


def _mask_refs(sq_ref, cc_ref, tri3_ref, tri_ref):
    return dict(bd_lc=sq_ref.at[0], bd_ll=sq_ref.at[1], seg=sq_ref.at[2], bd_cc=cc_ref,
                strict=tri3_ref.at[0], incl=tri3_ref.at[1], eye=tri3_ref.at[2], tri=tri_ref)


N_MASKS = 4
_VEC_ROWS = ("mu_r", "mu_k", "mu_v", "w0", "a0", "k_k", "k_a", "r_k", "gn_w", "gn_b")
_TAIL_NAMES = ("wl", "bonus", "gate")
_SLOT1_NAMES = ("ar", "bk", "vbd", "abd", "bkh", "v") + _TAIL_NAMES
_SLOT2_NAMES = ("q", "y0", "g", "n") + _TAIL_NAMES


def _slot_shapes(tt, L):
    C, n_c, hl = GROUP_COLS, tt // L, GROUP_HEADS * L
    return dict(ar=((n_c, 2 * L, C), BF16), bk=((n_c, 2 * hl, C), BF16), vbd=((n_c, hl, C), BF16),
                abd=((n_c, hl, C), BF16), bkh=((n_c, 2 * L, C), BF16), v=((tt, C), BF16),
                wl=((n_c, 1, C), F32), bonus=((tt, C), F32), gate=((tt, C), F32),
                q=((n_c, L, C), BF16), y0=((n_c, L, C), F32), g=((n_c, C, C), BF16),
                n=((n_c, C, C), F32))


def _shift_rows(x, carry_row):
    prev = pltpu.roll(x, 1, axis=0)
    return jnp.where(_iota(x.shape, 0) == 0, carry_row, prev)


def _bf(x):
    return x.astype(BF16)


def _cat(axis, *xs):
    return jnp.concatenate(xs, axis=axis)


def _tile_rows(x):
    return jnp.concatenate([x] * GROUP_HEADS, axis=0)


def _each(f, *lists):
    return [f(*xs) for xs in zip(*lists)]


def _lora_act_kernel(plo_ref, clo_ref, mulo_ref, act_ref, carry_scr):
    tt = plo_ref.shape[0]
    x = plo_ref[...].astype(F32)
    carry = jnp.where(pl.program_id(1) == 0, clo_ref[...], carry_scr[...])
    prev = _shift_rows(x, carry[15:16, :].astype(F32))
    lo = x + mulo_ref[...] * (prev - x)
    carry_scr[...] = plo_ref[tt - 16:tt, :]
    act_ref[:, :LORA_PAD] = _bf(jnp.tanh(lo[:, :LORA_PAD]))
    act_ref[:, LORA_PAD:2 * LORA_PAD] = _bf(lo[:, LORA_PAD:2 * LORA_PAD])
    act_ref[:, 2 * LORA_PAD:] = _bf(_sigmoid(lo[:, 2 * LORA_PAD:]))


def _lora_act(proj, carry16, mu_lo, *, bsz, seq, tt, off_lo):
    lo_w = 4 * LORA_PAD
    n_t = seq // tt
    lb = off_lo // lo_w
    return pl.pallas_call(
        _lora_act_kernel,
        grid=(bsz, n_t),
        in_specs=[pl.BlockSpec((tt, lo_w), lambda b, t: (b * n_t + t, lb)),
                  pl.BlockSpec((16, lo_w), lambda b, t: (0, lb)),
                  pl.BlockSpec((1, lo_w), lambda b, t: (0, 0))],
        out_specs=pl.BlockSpec((tt, lo_w), lambda b, t: (b * n_t + t, 0)),
        out_shape=jax.ShapeDtypeStruct((bsz * seq, lo_w), BF16),
        scratch_shapes=[pltpu.VMEM((16, lo_w), BF16)],
        compiler_params=_cparams(("parallel", "arbitrary")),
        name="lora_act",
    )(proj, carry16, mu_lo)


def _rwkv_prep(tp, gp, L, mk, refs, wr):
    p_ref, act_ref, c0_ref, vec_ref, wlo_ref, carry_scr = refs
    tt = p_ref.shape[0]
    C = GROUP_COLS
    vec = lambda name: vec_ref[_VEC_ROWS.index(name):_VEC_ROWS.index(name) + 1, :]
    carry = jnp.where(tp == 0, c0_ref[...], carry_scr[gp])

    def shifted(j, mu):
        cols = slice(j * C, (j + 1) * C)
        x = p_ref[:, cols].astype(F32)
        prev = _shift_rows(x, carry[15:16, cols].astype(F32))
        return x + mu * (prev - x)

    r = shifted(0, vec("mu_r"))
    k = shifted(1, vec("mu_k"))
    v = shifted(2, vec("mu_v"))
    carry_scr[gp] = p_ref[tt - 16:tt, :]
    kkr = k * vec("k_k")
    yield

    u = vec("w0") + _dot(act_ref[:, :LORA_PAD], wlo_ref[:LORA_PAD, :])
    lw = -np.exp(-0.5) * _sigmoid(u)
    alpha = _sigmoid(vec("a0") + _dot(act_ref[:, LORA_PAD:2 * LORA_PAD],
                                      wlo_ref[LORA_PAD:2 * LORA_PAD, :]))
    wr["gate"][...] = _dot(act_ref[:, 2 * LORA_PAD:], wlo_ref[2 * LORA_PAD:, :])
    kk = kkr * lax.rsqrt(jnp.maximum(_dot(_bf(kkr * kkr), mk["seg"][...]), 1e-24))
    k2 = k * (1.0 + (alpha - 1.0) * vec("k_a"))
    b = kk * alpha
    yield

    tri = mk["tri"][...]
    cum = _cat(0, *[_dot01(tri, lw[c * L:(c + 1) * L], 2) for c in range(tt // L)])
    wr["bonus"][...] = _dot(_bf(r * k2 * vec("r_k")), mk["seg"][...]) * v
    e_n = jnp.exp(-cum)
    a_t = _bf(-kk * jnp.exp(cum - lw))
    r_t = _bf(r * jnp.exp(cum))
    b_t = _bf(b * e_n)
    k_t = _bf(k2 * e_n)
    vb = _bf(v)
    wr["v"][...] = vb
    bd_lc = mk["bd_lc"][...]
    for c in range(tt // L):
        sl = slice(c * L, (c + 1) * L)
        cl = cum[sl.stop - 1:sl.stop, :]
        e_l = jnp.exp(cl - cum[sl])
        wr["wl"][c] = jnp.exp(cl)
        wr["ar"][c, :L] = a_t[sl]
        wr["ar"][c, L:] = r_t[sl]
        wr["bk"][c, :GROUP_HEADS * L] = _tile_rows(b_t[sl]) * bd_lc
        wr["bk"][c, GROUP_HEADS * L:] = _tile_rows(k_t[sl]) * bd_lc
        wr["vbd"][c] = _tile_rows(vb[sl]) * bd_lc
        wr["abd"][c] = _tile_rows(a_t[sl]) * bd_lc
        wr["bkh"][c, :L] = _bf(b[sl] * e_l)
        wr["bkh"][c, L:] = _bf(k2[sl] * e_l)


def _rwkv_local(L, mk, rd, wr):
    tt = rd["v"].shape[0]
    n_c = tt // L
    n_h = GROUP_HEADS * L
    C = GROUP_COLS
    chunks = [slice(c * L, (c + 1) * L) for c in range(n_c)]
    bd_lc, bd_ll = mk["bd_lc"][...], mk["bd_ll"][...]
    strict, incl = mk["strict"][...] > 0, mk["incl"][...] > 0
    bd_cc = mk["bd_cc"][...]
    bd = lambda xb: _tile_rows(xb) * bd_lc
    bd_sq = lambda pb: _tile_rows(pb) * bd_ll
    for k in _TAIL_NAMES:
        wr[k][...] = rd[k][...]

    ar = [rd["ar"][c] for c in range(n_c)]
    aa = _each(_dot_nt, ar, [rd["bk"][c] for c in range(n_c)])
    yield
    a_ab = _each(lambda x: jnp.where(strict, x[:L, :n_h], 0.0), aa)
    a_ak = _each(lambda x: _bf(jnp.where(strict, x[:L, n_h:], 0.0)), aa)
    a_rb = _each(lambda x: _bf(jnp.where(incl, x[L:, :n_h], 0.0)), aa)
    a_rk = _each(lambda x: _bf(jnp.where(incl, x[L:, n_h:], 0.0)), aa)

    n_steps = int(np.log2(L))
    eye = mk["eye"][...]
    tp = _each(lambda x: eye + x, a_ab)
    p = _each(lambda x: _dot(_bf(x), bd_sq(_bf(x))), a_ab)
    yield
    for _ in range(1, n_steps - 1):
        out = _each(lambda x, t: _dot(_bf(_cat(0, x, t)), bd_sq(_bf(x))), p, tp)
        yield
        p = _each(lambda o: o[:L], out)
        tp = _each(lambda t, o: t + o[L:], tp, out)
    tb = _each(lambda t, x: _bf(t + _dot(_bf(t), bd_sq(_bf(x)))), tp, p)
    yield

    v_bd = [rd["vbd"][c] for c in range(n_c)]
    x0 = _each(_dot, a_ak, v_bd)
    yield
    ut = _each(lambda t, x, c: _dot(t, _cat(1, bd(_bf(x)), rd["abd"][c])), tb, x0, range(n_c))
    yield
    u0 = _each(lambda x: _bf(x[:, :C]), ut)
    ta = _each(lambda x: _bf(x[:, C:]), ut)
    for c in range(n_c):
        wr["q"][c] = _bf(ar[c][L:].astype(F32) + _dot(a_rb[c], bd(ta[c])))
    yield
    for c in range(n_c):
        wr["y0"][c] = _dot(_cat(1, a_rb[c], a_rk[c]), _cat(0, bd(u0[c]), v_bd[c]))
    yield
    for c in range(n_c):
        bkh = rd["bkh"][c]
        wr["g"][c] = _bf(_dot_tn(ta[c], bkh[:L]) * bd_cc)
        wr["n"][c] = _dot_tn(_cat(0, u0[c], rd["v"][chunks[c]]), bkh) * bd_cc


def _rwkv_serial(tc, gc, L, mk, refs, rd, emit_state):
    s0_ref, vec_ref, y_ref, sout_ref, s_scr = refs
    gnw = vec_ref[_VEC_ROWS.index("gn_w"):_VEC_ROWS.index("gn_w") + 1, :]
    gnb = vec_ref[_VEC_ROWS.index("gn_b"):_VEC_ROWS.index("gn_b") + 1, :]
    n_c = rd["q"].shape[0]
    S = jnp.where(tc == 0, s0_ref[gc], s_scr[gc])
    ys = []
    for c in range(n_c):
        sb = _bf(S)
        ys.append(rd["y0"][c] + _dot_nt(rd["q"][c], sb))
        S = S * rd["wl"][c] + _dot(sb, rd["g"][c]) + rd["n"][c]
        yield
    s_scr[gc] = S
    if emit_state:
        sout_ref[0] = S
    y = ys[0] if n_c == 1 else _cat(0, *ys)

    inv_n = 1.0 / RWKV_HEAD
    seg = mk["seg"][...]
    mean = _dot(_bf(y), seg) * inv_n
    yield
    d = y - mean
    var = _dot(_bf(d * d), seg) * inv_n
    yield
    yn = d * lax.rsqrt(var + GN_EPS) * gnw + gnb
    y_ref[...] = ((yn + rd["bonus"][...]) * rd["gate"][...]).astype(y_ref.dtype)


def _weave(gens, order):
    for who in order:
        next(gens[who], None)
    for gen in gens.values():
        for _ in gen:
            pass


_WEAVE = "ppp" + "ls" * 12


def _rwkv_kernel(*refs, L, n_t, n_g, emit_state):
    n_prep = 5
    n_in = n_prep + N_MASKS + 2
    in_refs, rest = refs[:n_in], refs[n_in:]
    prep_in, chain_in = in_refs[:n_prep], in_refs[-2:]
    mk = _mask_refs(*in_refs[n_prep:n_prep + N_MASKS])
    if emit_state:
        y_ref, sout_ref = rest[:2]
        scr = rest[2:]
    else:
        y_ref, sout_ref = rest[0], None
        scr = rest[1:]
    s_scr, carry_scr = scr[:2]
    scr = list(scr[2:])
    slots1 = [dict(zip(_SLOT1_NAMES, [scr.pop(0) for _ in _SLOT1_NAMES])) for _ in range(2)]
    slots2 = [dict(zip(_SLOT2_NAMES, [scr.pop(0) for _ in _SLOT2_NAMES])) for _ in range(2)]

    s = pl.program_id(0)
    n = pl.num_programs(0) - 2
    sp = jnp.minimum(s, n - 1)
    sc = jnp.clip(s - 2, 0, n - 1)
    tp, gp = (sp // n_g) % n_t, sp % n_g
    tc, gc = (sc // n_g) % n_t, sc % n_g

    @pl.when(s == 0)
    def _():
        for ref in (tuple(slots1[1].values()) + tuple(slots2[0].values()) + tuple(slots2[1].values())
                    + (s_scr, carry_scr)):
            ref[...] = jnp.zeros_like(ref)

    def step(parity):
        gens = dict(
            p=_rwkv_prep(tp, gp, L, mk, prep_in + (carry_scr,), slots1[parity]),
            l=_rwkv_local(L, mk, slots1[1 - parity], slots2[1 - parity]),
            s=_rwkv_serial(tc, gc, L, mk, chain_in + (y_ref, sout_ref, s_scr), slots2[parity],
                           emit_state))
        _weave(gens, _WEAVE)

    for parity in range(2):
        pl.when(s % 2 == parity)(functools.partial(step, parity))


def _rwkv(proj, act, carry16, s0, rp, *, bsz, seq, tt, d, off_rkv, emit_state):
    C = GROUP_COLS
    n_g = d // C
    L = min(CHUNK, tt)
    n_t = seq // tt
    assert n_t == 1 or not emit_state
    n = bsz * n_t * n_g
    lo_w = 4 * LORA_PAD
    cb = off_rkv // (3 * C)
    masks = _wkv_masks(L)

    def tile(s):
        return s // n_g, s % n_g

    def prep_ix(f):
        return lambda s: f(*tile(jnp.minimum(s, n - 1)))

    def serial_ix(f):
        return lambda s: f(*tile(jnp.clip(s - 2, 0, n - 1)))

    const = lambda a: pl.BlockSpec(a.shape, lambda s, nd=a.ndim: (0,) * nd)
    n_vec = len(_VEC_ROWS)
    in_specs = [
        pl.BlockSpec((tt, 3 * C), prep_ix(lambda r, g: (r, cb + g))),
        pl.BlockSpec((tt, lo_w), prep_ix(lambda r, g: (r, 0))),
        pl.BlockSpec((16, 3 * C), prep_ix(lambda r, g: (0, cb + g))),
        pl.BlockSpec((n_vec, C), prep_ix(lambda r, g: (0, g))),
        pl.BlockSpec((lo_w, C), prep_ix(lambda r, g: (0, g))),
    ] + [const(m) for m in masks] + [
        const(s0),
        pl.BlockSpec((n_vec, C), serial_ix(lambda r, g: (0, g))),
    ]
    out_specs = [pl.BlockSpec((tt, C), serial_ix(lambda r, g: (r, g)))]
    out_shape = [jax.ShapeDtypeStruct((bsz * seq, d), BF16)]
    if emit_state:
        out_specs.append(pl.BlockSpec((1, C, C), serial_ix(lambda r, g: (r * n_g + g, 0, 0))))
        out_shape.append(jax.ShapeDtypeStruct((bsz * n_g, C, C), F32))
    shp = _slot_shapes(tt, L)
    scratch = [pltpu.VMEM((n_g, C, C), F32), pltpu.VMEM((n_g, 16, 3 * C), BF16)]
    scratch += [pltpu.VMEM(*shp[k]) for _ in range(2) for k in _SLOT1_NAMES]
    scratch += [pltpu.VMEM(*shp[k]) for _ in range(2) for k in _SLOT2_NAMES]
    return pl.pallas_call(
        functools.partial(_rwkv_kernel, L=L, n_t=n_t, n_g=n_g, emit_state=emit_state),
        grid=(n + 2,),
        in_specs=in_specs,
        out_specs=out_specs,
        out_shape=out_shape,
        scratch_shapes=scratch,
        compiler_params=_cparams(("arbitrary",)),
        name="rwkv",
    )(proj, act, carry16, rp["vec"], rp["w_lora"], *masks, s0, rp["vec"])


def _rope(x, c, s1, s2):
    return x * c + pltpu.roll(x, 96, axis=1) * s1 + pltpu.roll(x, 32, axis=1) * s2


def _mla_prep_kernel(cq_ref, ckv_ref, kpe_ref, qn_ref, kvn_ref, wq_ref, wkv_ref,
                     c_ref, s1_ref, s2_ref, q_ref, kn_ref, v_ref, kr_ref, *, n_heads):
    c, s1, s2 = c_ref[...], s1_ref[...], s2_ref[...]
    scale = QK_DIM ** -0.5 * np.log2(np.e)
    cq = _rms(cq_ref[...].astype(F32), qn_ref[...]).astype(BF16)
    for h in range(n_heads):
        qh = _dot(cq, wq_ref[:, h * QHEAD_COLS:(h + 1) * QHEAD_COLS]) * scale
        q_ref[:, h * QHEAD_COLS:h * QHEAD_COLS + LANES] = qh[:, :LANES].astype(BF16)
        q_ref[:, h * QHEAD_COLS + LANES:(h + 1) * QHEAD_COLS] = (
            _rope(qh[:, LANES:], c, s1, s2).astype(BF16))
    ckv = _rms(ckv_ref[...].astype(F32), kvn_ref[...]).astype(BF16)
    hw = n_heads * NOPE_DIM
    kn_ref[...] = _dot(ckv, wkv_ref[:, :hw]).astype(BF16)
    v_ref[...] = _dot(ckv, wkv_ref[:, hw:]).astype(BF16)
    kr_ref[...] = _rope(kpe_ref[...].astype(F32), c, s1, s2).astype(BF16)


def _mla_prep(proj, mp, tabs, *, seq, n_heads, off_cq, off_ckv, off_kpe):
    m = proj.shape[0]
    tm = min(512, seq)
    n_t = seq // tm
    ql = mp["w_uq"].shape[0]
    kl = mp["w_ukv"].shape[0]
    hw = n_heads * NOPE_DIM
    tab = pl.BlockSpec((tm, LANES), lambda i: (i % n_t, 0))
    full = lambda a: pl.BlockSpec(a.shape, lambda i: (0, 0))
    return pl.pallas_call(
        functools.partial(_mla_prep_kernel, n_heads=n_heads),
        grid=(m // tm,),
        in_specs=[pl.BlockSpec((tm, ql), lambda i: (i, off_cq // ql)),
                  pl.BlockSpec((tm, kl), lambda i: (i, off_ckv // kl)),
                  pl.BlockSpec((tm, LANES), lambda i: (i, off_kpe // LANES)),
                  full(mp["q_norm"]), full(mp["kv_norm"]), full(mp["w_uq"]), full(mp["w_ukv"]),
                  tab, tab, tab],
        out_specs=[pl.BlockSpec((tm, n_heads * QHEAD_COLS), lambda i: (i, 0)),
                   pl.BlockSpec((tm, hw), lambda i: (i, 0)),
                   pl.BlockSpec((tm, hw), lambda i: (i, 0)),
                   pl.BlockSpec((tm, LANES), lambda i: (i, 0))],
        out_shape=[jax.ShapeDtypeStruct((m, n_heads * QHEAD_COLS), BF16),
                   jax.ShapeDtypeStruct((m, hw), BF16),
                   jax.ShapeDtypeStruct((m, hw), BF16),
                   jax.ShapeDtypeStruct((m, LANES), BF16)],
        compiler_params=_cparams(("parallel",)),
        name="mla_prep",
    )(proj, proj, proj, mp["q_norm"], mp["kv_norm"], mp["w_uq"], mp["w_ukv"], *tabs)


def _attn_kernel(q_ref, kn_ref, kr_ref, v_ref, km_ref, vm_ref, o_ref, *, tq):
    seq = q_ref.shape[0]
    n_q = seq // tq
    qs = [q_ref[i * tq:(i + 1) * tq, :] for i in range(n_q)]
    kts = [jnp.concatenate([kn_ref[j * tq:(j + 1) * tq, :], kr_ref[j * tq:(j + 1) * tq, :]], axis=1)
           for j in range(n_q)]
    vts = [v_ref[j * tq:(j + 1) * tq, :] for j in range(n_q)]
    causal = _iota((tq, tq), 1) <= _iota((tq, tq), 0)

    ss = [_dot_nt(q, km_ref[0]) for q in qs]
    ms = [jnp.max(s, axis=-1, keepdims=True) for s in ss]
    ps = [jnp.exp2(s - m) for s, m in zip(ss, ms)]
    ls = [jnp.sum(p, axis=-1, keepdims=True) for p in ps]
    accs = [_dot(p.astype(BF16), vm_ref[0]) for p in ps]

    for d in range(n_q):
        act = list(range(d, n_q))
        ss = [_dot_nt(qs[i], kts[i - d]) for i in act]
        if d == 0:
            ss = [jnp.where(causal, s, -1e30) for s in ss]
        m_new = [jnp.maximum(ms[i], jnp.max(s, axis=-1, keepdims=True)) for i, s in zip(act, ss)]
        corr = [jnp.exp2(ms[i] - mn) for i, mn in zip(act, m_new)]
        ps = [jnp.exp2(s - mn) for s, mn in zip(ss, m_new)]
        pv = [_dot(p.astype(BF16), vts[i - d]) for i, p in zip(act, ps)]
        for n, i in enumerate(act):
            ms[i] = m_new[n]
            ls[i] = corr[n] * ls[i] + jnp.sum(ps[n], axis=-1, keepdims=True)
            accs[i] = corr[n] * accs[i] + pv[n]
    for i in range(n_q):
        o_ref[i * tq:(i + 1) * tq, :] = (accs[i] / ls[i]).astype(o_ref.dtype)


def _attn(q, kn, kr, v, k_meta, v_meta, *, bsz, seq, n_heads):
    tq = min(512, seq)
    return pl.pallas_call(
        functools.partial(_attn_kernel, tq=tq),
        grid=(bsz, n_heads),
        in_specs=[pl.BlockSpec((seq, QHEAD_COLS), lambda b, h: (b, h)),
                  pl.BlockSpec((seq, NOPE_DIM), lambda b, h: (b, h)),
                  pl.BlockSpec((seq, LANES), lambda b, h: (b, 0)),
                  pl.BlockSpec((seq, V_DIM), lambda b, h: (b, h)),
                  pl.BlockSpec((1, N_META, QHEAD_COLS), lambda b, h: (h, 0, 0)),
                  pl.BlockSpec((1, N_META, V_DIM), lambda b, h: (h, 0, 0))],
        out_specs=pl.BlockSpec((seq, V_DIM), lambda b, h: (b, h)),
        out_shape=jax.ShapeDtypeStruct((bsz * seq, n_heads * V_DIM), BF16),
        compiler_params=_cparams(("parallel", "parallel")),
        name="attn",
    )(q, kn, kr, v, k_meta, v_meta)


def _outproj_kernel(h_ref, ya_ref, yb_ref, ga_ref, gb_ref, w_ref, o_ref):
    mix = (_sigmoid(ga_ref[...].astype(F32)) * ya_ref[...].astype(F32)
           + _sigmoid(gb_ref[...].astype(F32)) * yb_ref[...].astype(F32))
    o_ref[...] = h_ref[...] + _dot(mix.astype(BF16), w_ref[...])


def _outproj(h, ya, yb, proj, w_out, *, off_gate):
    m, d = h.shape
    tm = min(512, m)
    gb = off_gate // d
    row = pl.BlockSpec((tm, d), lambda i: (i, 0))
    return pl.pallas_call(
        _outproj_kernel,
        grid=(m // tm,),
        in_specs=[row, row, row,
                  pl.BlockSpec((tm, d), lambda i: (i, gb)),
                  pl.BlockSpec((tm, d), lambda i: (i, gb + 1)),
                  pl.BlockSpec((d, d), lambda i: (0, 0))],
        out_specs=row,
        out_shape=jax.ShapeDtypeStruct((m, d), F32),
        compiler_params=_cparams(("parallel",)),
        name="outproj",
    )(h, ya, yb, proj, proj, w_out)


def _pad_cols(w, n):
    return jnp.pad(w, ((0, 0), (0, n - w.shape[1])))


def _pad_rows(w, n):
    return jnp.pad(w, ((0, n - w.shape[0]), (0, 0)))


def _rope_tables(pos):
    inv_freq = 1.0 / (ROPE_THETA ** (jnp.arange(0, ROPE_DIM, 2, dtype=F32) / ROPE_DIM))
    ang = pos.astype(F32)[:, None] * inv_freq[None, :]
    cos, sin = jnp.cos(ang), jnp.sin(ang)
    z = jnp.zeros_like(cos)
    return (jnp.concatenate([cos, cos, z, z], axis=1),
            jnp.concatenate([-sin, z, z, z], axis=1),
            jnp.concatenate([z, sin, z, z], axis=1))


def kernel(x, meta_tokens, ffn1_norm, ffn1_w_gate, ffn1_w_up, ffn1_w_down, mix_norm, w_in,
           tm_mu, w0, w_up, a0, a_up, g_up, k_k, k_a, r_k, gn_w, gn_b, q_norm, w_uq,
           kv_norm, w_ukv, w_out, ffn2_norm, ffn2_w_gate, ffn2_w_up, ffn2_w_down, final_norm):
    bsz, seq, d = x.shape
    assert d % GROUP_COLS == 0 and seq % 16 == 0
    assert ffn1_norm.shape[0] == 1, "single-layer stack"
    n_mla = d // 128
    w_lora, a_lora, g_lora = w_up.shape[1], a_up.shape[1], g_up.shape[1]
    q_lora, kv_lora = w_uq.shape[1], w_ukv.shape[1]
    assert w_lora <= LORA_PAD and a_lora <= LORA_PAD and g_lora == 2 * LORA_PAD
    rwkv_cols = 3 * d + w_lora + a_lora + g_lora
    mla_cols = q_lora + kv_lora + ROPE_DIM

    wi = w_in[0]
    o1, o2 = 3 * d, 3 * d + w_lora
    o3 = o2 + a_lora
    n_g = d // GROUP_COLS

    def group_major(a):
        lead = a.shape[:-1]
        a = a.reshape(lead + (3, n_g, GROUP_COLS))
        return jnp.swapaxes(a, -3, -2).reshape(lead + (3 * d,))

    wi_rkv = group_major(wi[:, :3 * d])
    wi_lo = jnp.concatenate([_pad_cols(wi[:, o1:o2], LORA_PAD), _pad_cols(wi[:, o2:o3], LORA_PAD),
                             wi[:, o3:rwkv_cols]], axis=1)
    m0 = rwkv_cols
    wi_cq = wi[:, m0:m0 + q_lora]
    wi_ckv = wi[:, m0 + q_lora:m0 + q_lora + kv_lora]
    wi_kpe = _pad_cols(wi[:, m0 + q_lora + kv_lora:m0 + mla_cols], LANES)
    wi_gate = wi[:, m0 + mla_cols:]
    parts = [wi_rkv, wi_gate, wi_lo, wi_cq, wi_ckv, wi_kpe]
    offs = np.cumsum([0] + [p.shape[1] for p in parts])
    off_rkv, off_gate, off_lo, off_cq, off_ckv, off_kpe, n_used = (int(o) for o in offs)
    n_proj = -(-n_used // 1024) * 1024
    parts.append(jnp.zeros((d, n_proj - n_used), wi.dtype))
    w_proj = jnp.concatenate([p.astype(BF16) for p in parts], axis=1)

    mu = tm_mu[0]
    vec = dict(mu_r=mu[None, :d], mu_k=mu[None, d:2 * d], mu_v=mu[None, 2 * d:3 * d], w0=w0, a0=a0,
               k_k=k_k, k_a=k_a, r_k=r_k[0].reshape(1, d), gn_w=gn_w, gn_b=gn_b)
    rp = dict(
        mu_lo=jnp.concatenate([jnp.pad(mu[o1:o2], (0, LORA_PAD - w_lora)),
                               jnp.pad(mu[o2:o3], (0, LORA_PAD - a_lora)), mu[o3:]])[None, :],
        vec=jnp.concatenate([vec[name] for name in _VEC_ROWS], axis=0),
        w_lora=jnp.concatenate([_pad_rows(w_up[0], LORA_PAD), _pad_rows(a_up[0], LORA_PAD),
                                g_up[0]], axis=0).astype(BF16),
    )

    wq = w_uq[0].reshape(q_lora, n_mla, QK_DIM)
    wq = jnp.pad(wq, ((0, 0), (0, 0), (0, QHEAD_COLS - QK_DIM))).reshape(q_lora, n_mla * QHEAD_COLS)
    wkv = w_ukv[0].reshape(kv_lora, n_mla, NOPE_DIM + V_DIM)
    wkv = jnp.concatenate([wkv[:, :, :NOPE_DIM].reshape(kv_lora, -1),
                           wkv[:, :, NOPE_DIM:].reshape(kv_lora, -1)], axis=1)
    mp = dict(q_norm=q_norm, kv_norm=kv_norm, w_uq=wq.astype(BF16), w_ukv=wkv.astype(BF16))

    f1 = (_to_bf16(ffn1_w_gate[0]), _to_bf16(ffn1_w_up[0]), _to_bf16(ffn1_w_down[0]))
    f2 = (_to_bf16(ffn2_w_gate[0]), _to_bf16(ffn2_w_up[0]), _to_bf16(ffn2_w_down[0]))
    w_o = w_out[0].astype(BF16)

    def front(h_rows, *, b, s, pos, carry16, s0, emit_state):
        h1 = _ffn(h_rows, ffn1_norm, *f1)
        proj = _inproj(h1, mix_norm, w_proj, BF16)
        tt = min(RWKV_TILE, s)
        act = _lora_act(proj, carry16, rp["mu_lo"], bsz=b, seq=s, tt=tt, off_lo=off_lo)
        rw = _rwkv(proj, act, carry16, s0, rp, bsz=b, seq=s, tt=tt, d=d, off_rkv=off_rkv,
                   emit_state=emit_state)
        q, kn, v, kr = _mla_prep(proj, mp, _rope_tables(pos), seq=s, n_heads=n_mla,
                                 off_cq=off_cq, off_ckv=off_ckv, off_kpe=off_kpe)
        return h1, proj, rw, (q, kn, kr, v)

    pad = META_ROWS - N_META
    meta_h = jnp.concatenate([jnp.zeros((pad, d), F32), meta_tokens.astype(F32)], axis=0)
    meta_pos = jnp.maximum(jnp.arange(META_ROWS) - pad, 0)
    _, proj_m, (_, s_meta), (_, kn_m, kr_m, v_m) = front(
        meta_h, b=1, s=META_ROWS, pos=meta_pos,
        carry16=jnp.zeros((16, n_proj), BF16), s0=jnp.zeros((n_g, GROUP_COLS, GROUP_COLS), F32),
        emit_state=True)
    kn_m = kn_m[pad:].reshape(N_META, n_mla, NOPE_DIM).transpose(1, 0, 2)
    kr_m = jnp.broadcast_to(kr_m[pad:][None], (n_mla, N_META, LANES))
    k_meta = jnp.concatenate([kn_m, kr_m], axis=2)
    v_meta = v_m[pad:].reshape(N_META, n_mla, V_DIM).transpose(1, 0, 2)

    h0 = x.reshape(bsz * seq, d)
    h1, proj, (ya,), (q, kn, kr, v) = front(
        h0, b=bsz, s=seq, pos=N_META + jnp.arange(seq),
        carry16=proj_m[META_ROWS - 16:], s0=s_meta, emit_state=False)
    yb = _attn(q, kn, kr, v, k_meta, v_meta, bsz=bsz, seq=seq, n_heads=n_mla)
    h2 = _outproj(h1, ya, yb, proj, w_o, off_gate=off_gate)
    y = _ffn(h2, ffn2_norm, *f2, final_norm[None, :])
    return y.reshape(bsz, seq, d)
```

```python
import functools

import numpy as np
import jax
import jax.numpy as jnp
from jax import lax
from jax.experimental import pallas as pl
from jax.experimental.pallas import tpu as pltpu

F32 = jnp.float32
BF16 = jnp.bfloat16

N_META = 16
NORM_EPS = 1e-6
RWKV_HEAD = 64
GN_EPS = RWKV_HEAD * 1e-5
NOPE_DIM = 128
ROPE_DIM = 64
V_DIM = 128
QK_DIM = NOPE_DIM + ROPE_DIM
ROPE_THETA = 10000.0

LANES = 128
GROUP_HEADS = 4
GROUP_COLS = GROUP_HEADS * RWKV_HEAD
CHUNK = 64
RWKV_TILE = 512
ATTN_TILE = 256
LORA_PAD = 128
QHEAD_COLS = 2 * LANES
META_ROWS = 64
VMEM_LIMIT = 56 * 1024 * 1024


def _cparams(sem):
    return pltpu.CompilerParams(dimension_semantics=sem, vmem_limit_bytes=VMEM_LIMIT)


def _sigmoid(x):
    return 0.5 * jnp.tanh(0.5 * x) + 0.5


def _dot(a, b):
    return jnp.dot(a, b, preferred_element_type=F32)


def _dot_nt(a, b):
    return lax.dot_general(a, b, (((1,), (1,)), ((), ())), preferred_element_type=F32)


def _dot_tn(a, b):
    return lax.dot_general(a, b, (((0,), (0,)), ((), ())), preferred_element_type=F32)


def _rms(x, g):
    ms = jnp.mean(x * x, axis=-1, keepdims=True)
    return x * lax.rsqrt(ms + NORM_EPS) * g


def _cast_kernel(w_ref, o_ref):
    o_ref[...] = w_ref[...].astype(o_ref.dtype)


def _to_bf16(w):
    k, n = w.shape
    tk = min(256, k)
    blk = pl.BlockSpec((tk, n), lambda i: (i, 0))
    return pl.pallas_call(
        _cast_kernel,
        grid=(k // tk,),
        in_specs=[blk],
        out_specs=blk,
        out_shape=jax.ShapeDtypeStruct((k, n), BF16),
        compiler_params=_cparams(("parallel",)),
        name="cast",
    )(w)


def _ffn_kernel(x_hbm, g_ref, wg_ref, wu_ref, wd_ref, *rest, final_norm):
    if final_norm:
        g2_ref, o_ref, xbuf, xn_scr, sem = rest
    else:
        o_ref, xbuf, xn_scr, sem = rest
    i, j = pl.program_id(0), pl.program_id(1)
    n_i, n_j = pl.num_programs(0), pl.num_programs(1)
    tm = xbuf.shape[0]

    def x_copy(tile):
        rows = pl.ds(pl.multiple_of(tile * tm, tm), tm)
        return pltpu.make_async_copy(x_hbm.at[rows, :], xbuf, sem)

    @pl.when(jnp.logical_and(i == 0, j == 0))
    def _():
        x_copy(0).start()

    def partial_sum():
        xn = xn_scr[...]
        a = _dot(xn, wg_ref[...])
        b = _dot(xn, wu_ref[...])
        mid = (a * _sigmoid(a) * b).astype(BF16)
        return _dot(mid, wd_ref[...])

    @pl.when(j == 0)
    def _():
        x_copy(i).wait()
        xn_scr[...] = _rms(xbuf[...], g_ref[...]).astype(BF16)
        o_ref[...] = 2.0 * xbuf[...] + partial_sum()

    @pl.when(j > 0)
    def _():
        o_ref[...] += partial_sum()

    @pl.when(jnp.logical_and(j == 1, i + 1 < n_i))
    def _():
        x_copy(i + 1).start()

    @pl.when(j == n_j - 1)
    def _():
        h = 0.5 * o_ref[...]
        o_ref[...] = _rms(h, g2_ref[...]) if final_norm else h


def _ffn(x, g, wg, wu, wd, g2=None):
    m, d = x.shape
    dff = wg.shape[1]
    tm = min(1024, m)
    tf = 512
    assert dff // tf >= 2
    row = pl.BlockSpec((tm, d), lambda i, j: (i, 0))
    vec = pl.BlockSpec((1, d), lambda i, j: (0, 0))
    args = [x, g, wg, wu, wd] + ([g2] if g2 is not None else [])
    return pl.pallas_call(
        functools.partial(_ffn_kernel, final_norm=g2 is not None),
        grid=(m // tm, dff // tf),
        in_specs=[pl.BlockSpec(memory_space=pl.ANY), vec,
                  pl.BlockSpec((d, tf), lambda i, j: (0, j)),
                  pl.BlockSpec((d, tf), lambda i, j: (0, j)),
                  pl.BlockSpec((tf, d), lambda i, j: (j, 0))] + ([vec] if g2 is not None else []),
        out_specs=row,
        out_shape=jax.ShapeDtypeStruct((m, d), F32),
        scratch_shapes=[pltpu.VMEM((tm, d), F32), pltpu.VMEM((tm, d), BF16),
                        pltpu.SemaphoreType.DMA(())],
        compiler_params=_cparams(("arbitrary", "arbitrary")),
        name="ffn",
    )(*args)


def _inproj_kernel(h_ref, g_ref, w_ref, o_ref, xn_scr):
    @pl.when(pl.program_id(1) == 0)
    def _():
        xn_scr[...] = _rms(h_ref[...], g_ref[...]).astype(BF16)

    o_ref[...] = _dot(xn_scr[...], w_ref[...]).astype(o_ref.dtype)


def _inproj(h, g, w, out_dtype):
    m, k = h.shape
    n = w.shape[1]
    tm = min(1024, m)
    tn = 2048 if n % 2048 == 0 else 1024
    return pl.pallas_call(
        _inproj_kernel,
        grid=(m // tm, n // tn),
        in_specs=[pl.BlockSpec((tm, k), lambda i, j: (i, 0)),
                  pl.BlockSpec((1, k), lambda i, j: (0, 0)),
                  pl.BlockSpec((k, tn), lambda i, j: (0, j))],
        out_specs=pl.BlockSpec((tm, tn), lambda i, j: (i, j)),
        out_shape=jax.ShapeDtypeStruct((m, n), out_dtype),
        scratch_shapes=[pltpu.VMEM((tm, k), BF16)],
        compiler_params=_cparams(("parallel", "arbitrary")),
        name="inproj",
    )(h, g, w)


def _iota(shape, dim):
    return lax.broadcasted_iota(jnp.int32, shape, dim)


def _split(x, parts):
    out = []
    for _ in range(parts - 1):
        hi = x.astype(BF16)
        out.append(hi)
        x = x - hi.astype(F32)
    out.append(x.astype(BF16))
    return out


def _dot01(m01, x, parts):
    return sum(_dot(m01, xp) for xp in _split(x, parts))


def _wkv_masks(L):
    C, H = GROUP_COLS, GROUP_HEADS
    hl = H * L
    r_hl, c_c = np.arange(hl)[:, None], np.arange(C)[None, :]
    t, s = np.arange(L)[:, None], np.arange(hl)[None, :] % L
    assert hl == C
    seg = np.arange(C)[:, None] // RWKV_HEAD == c_c // RWKV_HEAD
    sq = np.stack([r_hl // L == c_c // RWKV_HEAD, r_hl // L == np.arange(hl)[None, :] // L, seg])
    return (jnp.asarray(sq, BF16),
            jnp.asarray(seg, F32),
            jnp.asarray(np.stack([t > s, t >= s, t == s]), F32),
            jnp.asarray(np.arange(L)[:, None] >= np.arange(L)[None, :], BF16))


def _mask_refs(sq_ref, cc_ref, tri3_ref, tri_ref):
    return dict(bd_lc=sq_ref.at[0], bd_ll=sq_ref.at[1], seg=sq_ref.at[2], bd_cc=cc_ref,
                strict=tri3_ref.at[0], incl=tri3_ref.at[1], eye=tri3_ref.at[2], tri=tri_ref)


N_MASKS = 4
_VEC_ROWS = ("mu_r", "mu_k", "mu_v", "w0", "a0", "k_k", "k_a", "r_k", "gn_w", "gn_b")
_TAIL_NAMES = ("wl", "bonus", "gate")
_SLOT1_NAMES = ("ar", "bk", "vbd", "abd", "bkh", "v") + _TAIL_NAMES
_SLOT2_NAMES = ("q", "y0", "g", "n") + _TAIL_NAMES


def _slot_shapes(tt, L):
    C, n_c, hl = GROUP_COLS, tt // L, GROUP_HEADS * L
    return dict(ar=((n_c, 2 * L, C), BF16), bk=((n_c, 2 * hl, C), BF16), vbd=((n_c, hl, C), BF16),
                abd=((n_c, hl, C), BF16), bkh=((n_c, 2 * L, C), BF16), v=((tt, C), BF16),
                wl=((n_c, 1, C), F32), bonus=((tt, C), F32), gate=((tt, C), F32),
                q=((n_c, L, C), BF16), y0=((n_c, L, C), F32), g=((n_c, C, C), BF16),
                n=((n_c, C, C), F32))


def _shift_rows(x, carry_row):
    prev = pltpu.roll(x, 1, axis=0)
    return jnp.where(_iota(x.shape, 0) == 0, carry_row, prev)


def _bf(x):
    return x.astype(BF16)


def _cat(axis, *xs):
    return jnp.concatenate(xs, axis=axis)


def _tile_rows(x):
    return jnp.concatenate([x] * GROUP_HEADS, axis=0)


def _each(f, *lists):
    return [f(*xs) for xs in zip(*lists)]


def _lora_act_kernel(plo_ref, clo_ref, mulo_ref, act_ref, carry_scr):
    tt = plo_ref.shape[0]
    x = plo_ref[...].astype(F32)
    carry = jnp.where(pl.program_id(1) == 0, clo_ref[...], carry_scr[...])
    prev = _shift_rows(x, carry[15:16, :].astype(F32))
    lo = x + mulo_ref[...] * (prev - x)
    carry_scr[...] = plo_ref[tt - 16:tt, :]
    act_ref[:, :LORA_PAD] = _bf(jnp.tanh(lo[:, :LORA_PAD]))
    act_ref[:, LORA_PAD:2 * LORA_PAD] = _bf(lo[:, LORA_PAD:2 * LORA_PAD])
    act_ref[:, 2 * LORA_PAD:] = _bf(_sigmoid(lo[:, 2 * LORA_PAD:]))


def _lora_act(proj, carry16, mu_lo, *, bsz, seq, tt, off_lo):
    lo_w = 4 * LORA_PAD
    n_t = seq // tt
    lb = off_lo // lo_w
    return pl.pallas_call(
        _lora_act_kernel,
        grid=(bsz, n_t),
        in_specs=[pl.BlockSpec((tt, lo_w), lambda b, t: (b * n_t + t, lb)),
                  pl.BlockSpec((16, lo_w), lambda b, t: (0, lb)),
                  pl.BlockSpec((1, lo_w), lambda b, t: (0, 0))],
        out_specs=pl.BlockSpec((tt, lo_w), lambda b, t: (b * n_t + t, 0)),
        out_shape=jax.ShapeDtypeStruct((bsz * seq, lo_w), BF16),
        scratch_shapes=[pltpu.VMEM((16, lo_w), BF16)],
        compiler_params=_cparams(("parallel", "arbitrary")),
        name="lora_act",
    )(proj, carry16, mu_lo)


def _rwkv_prep(tp, gp, L, mk, refs, wr):
    p_refs, c0_refs = refs[0:3], refs[3:6]
    act_ref, vec_ref, wlo_ref, carry_scr = refs[6:]
    tt = act_ref.shape[0]
    vec = lambda name: vec_ref[_VEC_ROWS.index(name):_VEC_ROWS.index(name) + 1, :]

    def shifted(j, mu):
        x = p_refs[j][...].astype(F32)
        carry = jnp.where(tp == 0, c0_refs[j][...], carry_scr[3 * gp + j])
        prev = _shift_rows(x, carry[15:16, :].astype(F32))
        carry_scr[3 * gp + j] = p_refs[j][tt - 16:tt, :]
        return x + mu * (prev - x)

    r = shifted(0, vec("mu_r"))
    k = shifted(1, vec("mu_k"))
    v = shifted(2, vec("mu_v"))
    kkr = k * vec("k_k")
    yield

    u = vec("w0") + _dot(act_ref[:, :LORA_PAD], wlo_ref[:LORA_PAD, :])
    lw = -np.exp(-0.5) * _sigmoid(u)
    alpha = _sigmoid(vec("a0") + _dot(act_ref[:, LORA_PAD:2 * LORA_PAD],
                                      wlo_ref[LORA_PAD:2 * LORA_PAD, :]))
    wr["gate"][...] = _dot(act_ref[:, 2 * LORA_PAD:], wlo_ref[2 * LORA_PAD:, :])
    kk = kkr * lax.rsqrt(jnp.maximum(_dot(_bf(kkr * kkr), mk["seg"][...]), 1e-24))
    k2 = k * (1.0 + (alpha - 1.0) * vec("k_a"))
    b = kk * alpha
    yield

    tri = mk["tri"][...]
    cum = _cat(0, *[_dot01(tri, lw[c * L:(c + 1) * L], 2) for c in range(tt // L)])
    wr["bonus"][...] = _dot(_bf(r * k2 * vec("r_k")), mk["seg"][...]) * v
    e_n = jnp.exp(-cum)
    a_t = _bf(-kk * jnp.exp(cum - lw))
    r_t = _bf(r * jnp.exp(cum))
    b_t = _bf(b * e_n)
    k_t = _bf(k2 * e_n)
    vb = _bf(v)
    wr["v"][...] = vb
    bd_lc = mk["bd_lc"][...]
    for c in range(tt // L):
        sl = slice(c * L, (c + 1) * L)
        cl = cum[sl.stop - 1:sl.stop, :]
        e_l = jnp.exp(cl - cum[sl])
        wr["wl"][c] = jnp.exp(cl)
        wr["ar"][c, :L] = a_t[sl]
        wr["ar"][c, L:] = r_t[sl]
        wr["bk"][c, :GROUP_HEADS * L] = _tile_rows(b_t[sl]) * bd_lc
        wr["bk"][c, GROUP_HEADS * L:] = _tile_rows(k_t[sl]) * bd_lc
        wr["vbd"][c] = _tile_rows(vb[sl]) * bd_lc
        wr["abd"][c] = _tile_rows(a_t[sl]) * bd_lc
        wr["bkh"][c, :L] = _bf(b[sl] * e_l)
        wr["bkh"][c, L:] = _bf(k2[sl] * e_l)


def _rwkv_local(L, mk, rd, wr):
    tt = rd["v"].shape[0]
    n_c = tt // L
    n_h = GROUP_HEADS * L
    C = GROUP_COLS
    chunks = [slice(c * L, (c + 1) * L) for c in range(n_c)]
    bd_lc, bd_ll = mk["bd_lc"][...], mk["bd_ll"][...]
    strict, incl = mk["strict"][...] > 0, mk["incl"][...] > 0
    bd_cc = mk["bd_cc"][...]
    bd = lambda xb: _tile_rows(xb) * bd_lc
    bd_sq = lambda pb: _tile_rows(pb) * bd_ll
    for k in _TAIL_NAMES:
        wr[k][...] = rd[k][...]

    ar = [rd["ar"][c] for c in range(n_c)]
    aa = _each(_dot_nt, ar, [rd["bk"][c] for c in range(n_c)])
    yield
    a_ab = _each(lambda x: jnp.where(strict, x[:L, :n_h], 0.0), aa)
    a_ak = _each(lambda x: _bf(jnp.where(strict, x[:L, n_h:], 0.0)), aa)
    a_rb = _each(lambda x: _bf(jnp.where(incl, x[L:, :n_h], 0.0)), aa)
    a_rk = _each(lambda x: _bf(jnp.where(incl, x[L:, n_h:], 0.0)), aa)

    n_steps = int(np.log2(L))
    eye = mk["eye"][...]
    tp = _each(lambda x: eye + x, a_ab)
    p = _each(lambda x: _dot(_bf(x), bd_sq(_bf(x))), a_ab)
    yield
    for _ in range(1, n_steps - 1):
        out = _each(lambda x, t: _dot(_bf(_cat(0, x, t)), bd_sq(_bf(x))), p, tp)
        yield
        p = _each(lambda o: o[:L], out)
        tp = _each(lambda t, o: t + o[L:], tp, out)
    tb = _each(lambda t, x: _bf(t + _dot(_bf(t), bd_sq(_bf(x)))), tp, p)
    yield

    v_bd = [rd["vbd"][c] for c in range(n_c)]
    x0 = _each(_dot, a_ak, v_bd)
    yield
    ut = _each(lambda t, x, c: _dot(t, _cat(1, bd(_bf(x)), rd["abd"][c])), tb, x0, range(n_c))
    yield
    u0 = _each(lambda x: _bf(x[:, :C]), ut)
    ta = _each(lambda x: _bf(x[:, C:]), ut)
    for c in range(n_c):
        wr["q"][c] = _bf(ar[c][L:].astype(F32) + _dot(a_rb[c], bd(ta[c])))
    yield
    for c in range(n_c):
        wr["y0"][c] = _dot(_cat(1, a_rb[c], a_rk[c]), _cat(0, bd(u0[c]), v_bd[c]))
    yield
    for c in range(n_c):
        bkh = rd["bkh"][c]
        wr["g"][c] = _bf(_dot_tn(ta[c], bkh[:L]) * bd_cc)
        wr["n"][c] = _dot_tn(_cat(0, u0[c], rd["v"][chunks[c]]), bkh) * bd_cc


def _rwkv_serial(tc, gc, L, mk, refs, rd, emit_state):
    s0_ref, vec_ref, y_ref, sout_ref, s_scr = refs
    gnw = vec_ref[_VEC_ROWS.index("gn_w"):_VEC_ROWS.index("gn_w") + 1, :]
    gnb = vec_ref[_VEC_ROWS.index("gn_b"):_VEC_ROWS.index("gn_b") + 1, :]
    n_c = rd["q"].shape[0]
    S = jnp.where(tc == 0, s0_ref[gc], s_scr[gc])
    ys = []
    for c in range(n_c):
        sb = _bf(S)
        ys.append(rd["y0"][c] + _dot_nt(rd["q"][c], sb))
        S = S * rd["wl"][c] + _dot(sb, rd["g"][c]) + rd["n"][c]
        yield
    s_scr[gc] = S
    if emit_state:
        sout_ref[0] = S
    y = ys[0] if n_c == 1 else _cat(0, *ys)

    inv_n = 1.0 / RWKV_HEAD
    seg = mk["seg"][...]
    mean = _dot(_bf(y), seg) * inv_n
    yield
    d = y - mean
    var = _dot(_bf(d * d), seg) * inv_n
    yield
    yn = d * lax.rsqrt(var + GN_EPS) * gnw + gnb
    y_ref[...] = ((yn + rd["bonus"][...]) * rd["gate"][...]).astype(y_ref.dtype)


def _weave(gens, order):
    for who in order:
        next(gens[who], None)
    for gen in gens.values():
        for _ in gen:
            pass


_WEAVE = "ppp" + "ls" * 12


def _rwkv_kernel(*refs, L, n_t, n_g, emit_state):
    n_prep = 9
    n_in = n_prep + N_MASKS + 2
    in_refs, rest = refs[:n_in], refs[n_in:]
    prep_in, chain_in = in_refs[:n_prep], in_refs[-2:]
    mk = _mask_refs(*in_refs[n_prep:n_prep + N_MASKS])
    if emit_state:
        y_ref, sout_ref = rest[:2]
        scr = rest[2:]
    else:
        y_ref, sout_ref = rest[0], None
        scr = rest[1:]
    s_scr, carry_scr = scr[:2]
    scr = list(scr[2:])
    slots1 = [dict(zip(_SLOT1_NAMES, [scr.pop(0) for _ in _SLOT1_NAMES])) for _ in range(2)]
    slots2 = [dict(zip(_SLOT2_NAMES, [scr.pop(0) for _ in _SLOT2_NAMES])) for _ in range(2)]

    s = pl.program_id(0)
    n = pl.num_programs(0) - 2
    sp = jnp.minimum(s, n - 1)
    sc = jnp.clip(s - 2, 0, n - 1)
    tp, gp = (sp // n_g) % n_t, sp % n_g
    tc, gc = (sc // n_g) % n_t, sc % n_g

    @pl.when(s == 0)
    def _():
        for ref in (tuple(slots1[1].values()) + tuple(slots2[0].values()) + tuple(slots2[1].values())
                    + (s_scr, carry_scr)):
            ref[...] = jnp.zeros_like(ref)

    def step(parity):
        gens = dict(
            p=_rwkv_prep(tp, gp, L, mk, prep_in + (carry_scr,), slots1[parity]),
            l=_rwkv_local(L, mk, slots1[1 - parity], slots2[1 - parity]),
            s=_rwkv_serial(tc, gc, L, mk, chain_in + (y_ref, sout_ref, s_scr), slots2[parity],
                           emit_state))
        _weave(gens, _WEAVE)

    for parity in range(2):
        pl.when(s % 2 == parity)(functools.partial(step, parity))


def _rwkv(proj, act, carry16, s0, rp, *, bsz, seq, tt, d, off_rkv, emit_state):
    C = GROUP_COLS
    n_g = d // C
    L = min(CHUNK, tt)
    n_t = seq // tt
    assert n_t == 1 or not emit_state
    n = bsz * n_t * n_g
    lo_w = 4 * LORA_PAD
    cb = off_rkv // C
    masks = _wkv_masks(L)

    def tile(s):
        return s // n_g, s % n_g

    def prep_ix(f):
        return lambda s: f(*tile(jnp.minimum(s, n - 1)))

    def serial_ix(f):
        return lambda s: f(*tile(jnp.clip(s - 2, 0, n - 1)))

    const = lambda a: pl.BlockSpec(a.shape, lambda s, nd=a.ndim: (0,) * nd)
    n_vec = len(_VEC_ROWS)
    rkv = lambda rows: [pl.BlockSpec((rows, C), prep_ix(lambda r, g, j=j: (r if rows == tt else 0,
                                                                        cb + j * n_g + g)))
                        for j in range(3)]
    in_specs = rkv(tt) + rkv(16) + [
        pl.BlockSpec((tt, lo_w), prep_ix(lambda r, g: (r, 0))),
        pl.BlockSpec((n_vec, C), prep_ix(lambda r, g: (0, g))),
        pl.BlockSpec((lo_w, C), prep_ix(lambda r, g: (0, g))),
    ] + [const(m) for m in masks] + [
        const(s0),
        pl.BlockSpec((n_vec, C), serial_ix(lambda r, g: (0, g))),
    ]
    out_specs = [pl.BlockSpec((tt, C), serial_ix(lambda r, g: (r, g)))]
    out_shape = [jax.ShapeDtypeStruct((bsz * seq, d), BF16)]
    if emit_state:
        out_specs.append(pl.BlockSpec((1, C, C), serial_ix(lambda r, g: (r * n_g + g, 0, 0))))
        out_shape.append(jax.ShapeDtypeStruct((bsz * n_g, C, C), F32))
    shp = _slot_shapes(tt, L)
    scratch = [pltpu.VMEM((n_g, C, C), F32), pltpu.VMEM((3 * n_g, 16, C), BF16)]
    scratch += [pltpu.VMEM(*shp[k]) for _ in range(2) for k in _SLOT1_NAMES]
    scratch += [pltpu.VMEM(*shp[k]) for _ in range(2) for k in _SLOT2_NAMES]
    return pl.pallas_call(
        functools.partial(_rwkv_kernel, L=L, n_t=n_t, n_g=n_g, emit_state=emit_state),
        grid=(n + 2,),
        in_specs=in_specs,
        out_specs=out_specs,
        out_shape=out_shape,
        scratch_shapes=scratch,
        compiler_params=_cparams(("arbitrary",)),
        name="rwkv",
    )(proj, proj, proj, carry16, carry16, carry16, act, rp["vec"], rp["w_lora"], *masks, s0,
      rp["vec"])


def _rope(x, c, s1, s2):
    return x * c + pltpu.roll(x, 96, axis=1) * s1 + pltpu.roll(x, 32, axis=1) * s2


def _mla_prep_kernel(cq_ref, ckv_ref, kpe_ref, qn_ref, kvn_ref, wq_ref, wkv_ref,
                     c_ref, s1_ref, s2_ref, q_ref, kn_ref, v_ref, kr_ref, *, n_heads):
    c, s1, s2 = c_ref[...], s1_ref[...], s2_ref[...]
    scale = QK_DIM ** -0.5 * np.log2(np.e)
    cq = _rms(cq_ref[...].astype(F32), qn_ref[...]).astype(BF16)
    for h in range(n_heads):
        qh = _dot(cq, wq_ref[:, h * QHEAD_COLS:(h + 1) * QHEAD_COLS]) * scale
        q_ref[:, h * QHEAD_COLS:h * QHEAD_COLS + LANES] = qh[:, :LANES].astype(BF16)
        q_ref[:, h * QHEAD_COLS + LANES:(h + 1) * QHEAD_COLS] = (
            _rope(qh[:, LANES:], c, s1, s2).astype(BF16))
    ckv = _rms(ckv_ref[...].astype(F32), kvn_ref[...]).astype(BF16)
    hw = n_heads * NOPE_DIM
    kn_ref[...] = _dot(ckv, wkv_ref[:, :hw]).astype(BF16)
    v_ref[...] = _dot(ckv, wkv_ref[:, hw:]).astype(BF16)
    kr_ref[...] = _rope(kpe_ref[...].astype(F32), c, s1, s2).astype(BF16)


def _mla_prep(proj, mp, tabs, *, seq, n_heads, off_cq, off_ckv, off_kpe):
    m = proj.shape[0]
    tm = min(512, seq)
    n_t = seq // tm
    ql = mp["w_uq"].shape[0]
    kl = mp["w_ukv"].shape[0]
    hw = n_heads * NOPE_DIM
    tab = pl.BlockSpec((tm, LANES), lambda i: (i % n_t, 0))
    full = lambda a: pl.BlockSpec(a.shape, lambda i: (0, 0))
    return pl.pallas_call(
        functools.partial(_mla_prep_kernel, n_heads=n_heads),
        grid=(m // tm,),
        in_specs=[pl.BlockSpec((tm, ql), lambda i: (i, off_cq // ql)),
                  pl.BlockSpec((tm, kl), lambda i: (i, off_ckv // kl)),
                  pl.BlockSpec((tm, LANES), lambda i: (i, off_kpe // LANES)),
                  full(mp["q_norm"]), full(mp["kv_norm"]), full(mp["w_uq"]), full(mp["w_ukv"]),
                  tab, tab, tab],
        out_specs=[pl.BlockSpec((tm, n_heads * QHEAD_COLS), lambda i: (i, 0)),
                   pl.BlockSpec((tm, hw), lambda i: (i, 0)),
                   pl.BlockSpec((tm, hw), lambda i: (i, 0)),
                   pl.BlockSpec((tm, LANES), lambda i: (i, 0))],
        out_shape=[jax.ShapeDtypeStruct((m, n_heads * QHEAD_COLS), BF16),
                   jax.ShapeDtypeStruct((m, hw), BF16),
                   jax.ShapeDtypeStruct((m, hw), BF16),
                   jax.ShapeDtypeStruct((m, LANES), BF16)],
        compiler_params=_cparams(("parallel",)),
        name="mla_prep",
    )(proj, proj, proj, mp["q_norm"], mp["kv_norm"], mp["w_uq"], mp["w_ukv"], *tabs)


def _attn_kernel(q_ref, kn_ref, kr_ref, v_ref, km_ref, vmt_ref, o_ref, *, tq):
    seq = q_ref.shape[0]
    n_q = seq // tq
    mp = km_ref.shape[1]
    kx = jnp.concatenate([km_ref[0], jnp.concatenate([kn_ref[...], kr_ref[...]], axis=1)], axis=0)
    vxt = jnp.concatenate([vmt_ref[0], v_ref[...].astype(F32).T.astype(BF16)], axis=1)
    qs = [q_ref[i * tq:(i + 1) * tq, :] for i in range(n_q)]
    n_k = [mp + tq * (i + 1) for i in range(n_q)]
    meta_ok = _iota((mp, tq), 0) < N_META
    causal = _iota((tq, tq), 0) <= _iota((tq, tq), 1)

    ss = [_dot_nt(kx[:n], q) for n, q in zip(n_k, qs)]
    blocks = [[jnp.where(meta_ok, s[:mp], -1e30)] + ([s[mp:n - tq]] if n - tq > mp else [])
              + [jnp.where(causal, s[n - tq:], -1e30)] for n, s in zip(n_k, ss)]
    col_max = lambda x: jnp.max(x, axis=0, keepdims=True)
    ms = [functools.reduce(jnp.maximum, [col_max(x) for x in bl]) for bl in blocks]
    ps = [[jnp.exp2(x - m) for x in bl] for bl, m in zip(blocks, ms)]
    ls = [sum(jnp.sum(x, axis=0, keepdims=True) for x in pl_) for pl_ in ps]
    pv = [_dot(vxt[:, :n], jnp.concatenate([x.astype(BF16) for x in pl_], axis=0))
          for n, pl_ in zip(n_k, ps)]
    for i in range(n_q):
        o_ref[i * tq:(i + 1) * tq, :] = (pv[i] / ls[i]).T.astype(o_ref.dtype)


def _attn(q, kn, kr, v, k_meta, vt_meta, *, bsz, seq, n_heads):
    tq = min(ATTN_TILE, seq)
    mp = k_meta.shape[1]
    return pl.pallas_call(
        functools.partial(_attn_kernel, tq=tq),
        grid=(bsz, n_heads),
        in_specs=[pl.BlockSpec((seq, QHEAD_COLS), lambda b, h: (b, h)),
                  pl.BlockSpec((seq, NOPE_DIM), lambda b, h: (b, h)),
                  pl.BlockSpec((seq, LANES), lambda b, h: (b, 0)),
                  pl.BlockSpec((seq, V_DIM), lambda b, h: (b, h)),
                  pl.BlockSpec((1, mp, QHEAD_COLS), lambda b, h: (h, 0, 0)),
                  pl.BlockSpec((1, V_DIM, mp), lambda b, h: (h, 0, 0))],
        out_specs=pl.BlockSpec((seq, V_DIM), lambda b, h: (b, h)),
        out_shape=jax.ShapeDtypeStruct((bsz * seq, n_heads * V_DIM), BF16),
        compiler_params=_cparams(("parallel", "parallel")),
        name="attn",
    )(q, kn, kr, v, k_meta, vt_meta)


def _outproj_kernel(h_ref, ya_ref, yb_ref, ga_ref, gb_ref, w_ref, o_ref):
    mix = (_sigmoid(ga_ref[...].astype(F32)) * ya_ref[...].astype(F32)
           + _sigmoid(gb_ref[...].astype(F32)) * yb_ref[...].astype(F32))
    o_ref[...] = h_ref[...] + _dot(mix.astype(BF16), w_ref[...])


def _outproj(h, ya, yb, proj, w_out, *, off_gate):
    m, d = h.shape
    tm = min(512, m)
    gb = off_gate // d
    row = pl.BlockSpec((tm, d), lambda i: (i, 0))
    return pl.pallas_call(
        _outproj_kernel,
        grid=(m // tm,),
        in_specs=[row, row, row,
                  pl.BlockSpec((tm, d), lambda i: (i, gb)),
                  pl.BlockSpec((tm, d), lambda i: (i, gb + 1)),
                  pl.BlockSpec((d, d), lambda i: (0, 0))],
        out_specs=row,
        out_shape=jax.ShapeDtypeStruct((m, d), F32),
        compiler_params=_cparams(("parallel",)),
        name="outproj",
    )(h, ya, yb, proj, proj, w_out)


def _pad_cols(w, n):
    return jnp.pad(w, ((0, 0), (0, n - w.shape[1])))


def _pad_rows(w, n):
    return jnp.pad(w, ((0, n - w.shape[0]), (0, 0)))


def _rope_tables(pos):
    inv_freq = 1.0 / (ROPE_THETA ** (jnp.arange(0, ROPE_DIM, 2, dtype=F32) / ROPE_DIM))
    ang = pos.astype(F32)[:, None] * inv_freq[None, :]
    cos, sin = jnp.cos(ang), jnp.sin(ang)
    z = jnp.zeros_like(cos)
    return (jnp.concatenate([cos, cos, z, z], axis=1),
            jnp.concatenate([-sin, z, z, z], axis=1),
            jnp.concatenate([z, sin, z, z], axis=1))


def kernel(x, meta_tokens, ffn1_norm, ffn1_w_gate, ffn1_w_up, ffn1_w_down, mix_norm, w_in,
           tm_mu, w0, w_up, a0, a_up, g_up, k_k, k_a, r_k, gn_w, gn_b, q_norm, w_uq,
           kv_norm, w_ukv, w_out, ffn2_norm, ffn2_w_gate, ffn2_w_up, ffn2_w_down, final_norm):
    bsz, seq, d = x.shape
    assert d % GROUP_COLS == 0 and seq % 16 == 0
    assert ffn1_norm.shape[0] == 1, "single-layer stack"
    n_mla = d // 128
    w_lora, a_lora, g_lora = w_up.shape[1], a_up.shape[1], g_up.shape[1]
    q_lora, kv_lora = w_uq.shape[1], w_ukv.shape[1]
    assert w_lora <= LORA_PAD and a_lora <= LORA_PAD and g_lora == 2 * LORA_PAD
    rwkv_cols = 3 * d + w_lora + a_lora + g_lora
    mla_cols = q_lora + kv_lora + ROPE_DIM

    wi = w_in[0]
    o1, o2 = 3 * d, 3 * d + w_lora
    o3 = o2 + a_lora
    wi_rkv = wi[:, :3 * d]
    wi_lo = jnp.concatenate([_pad_cols(wi[:, o1:o2], LORA_PAD), _pad_cols(wi[:, o2:o3], LORA_PAD),
                             wi[:, o3:rwkv_cols]], axis=1)
    m0 = rwkv_cols
    wi_cq = wi[:, m0:m0 + q_lora]
    wi_ckv = wi[:, m0 + q_lora:m0 + q_lora + kv_lora]
    wi_kpe = _pad_cols(wi[:, m0 + q_lora + kv_lora:m0 + mla_cols], LANES)
    wi_gate = wi[:, m0 + mla_cols:]
    parts = [wi_rkv, wi_gate, wi_lo, wi_cq, wi_ckv, wi_kpe]
    offs = np.cumsum([0] + [p.shape[1] for p in parts])
    off_rkv, off_gate, off_lo, off_cq, off_ckv, off_kpe, n_used = (int(o) for o in offs)
    n_proj = -(-n_used // 1024) * 1024
    parts.append(jnp.zeros((d, n_proj - n_used), wi.dtype))
    w_proj = jnp.concatenate([p.astype(BF16) for p in parts], axis=1)

    mu = tm_mu[0]
    vec = dict(mu_r=mu[None, :d], mu_k=mu[None, d:2 * d], mu_v=mu[None, 2 * d:3 * d], w0=w0, a0=a0,
               k_k=k_k, k_a=k_a, r_k=r_k[0].reshape(1, d), gn_w=gn_w, gn_b=gn_b)
    rp = dict(
        mu_lo=jnp.concatenate([jnp.pad(mu[o1:o2], (0, LORA_PAD - w_lora)),
                               jnp.pad(mu[o2:o3], (0, LORA_PAD - a_lora)), mu[o3:]])[None, :],
        vec=jnp.concatenate([vec[name] for name in _VEC_ROWS], axis=0),
        w_lora=jnp.concatenate([_pad_rows(w_up[0], LORA_PAD), _pad_rows(a_up[0], LORA_PAD),
                                g_up[0]], axis=0).astype(BF16),
    )

    wq = w_uq[0].reshape(q_lora, n_mla, QK_DIM)
    wq = jnp.pad(wq, ((0, 0), (0, 0), (0, QHEAD_COLS - QK_DIM))).reshape(q_lora, n_mla * QHEAD_COLS)
    wkv = w_ukv[0].reshape(kv_lora, n_mla, NOPE_DIM + V_DIM)
    wkv = jnp.concatenate([wkv[:, :, :NOPE_DIM].reshape(kv_lora, -1),
                           wkv[:, :, NOPE_DIM:].reshape(kv_lora, -1)], axis=1)
    mp = dict(q_norm=q_norm, kv_norm=kv_norm, w_uq=wq.astype(BF16), w_ukv=wkv.astype(BF16))

    f1 = (_to_bf16(ffn1_w_gate[0]), _to_bf16(ffn1_w_up[0]), _to_bf16(ffn1_w_down[0]))
    f2 = (_to_bf16(ffn2_w_gate[0]), _to_bf16(ffn2_w_up[0]), _to_bf16(ffn2_w_down[0]))
    w_o = w_out[0].astype(BF16)

    def front(h_rows, *, b, s, pos, carry16, s0, emit_state):
        h1 = _ffn(h_rows, ffn1_norm, *f1)
        proj = _inproj(h1, mix_norm, w_proj, BF16)
        tt = min(RWKV_TILE, s)
        act = _lora_act(proj, carry16, rp["mu_lo"], bsz=b, seq=s, tt=tt, off_lo=off_lo)
        rw = _rwkv(proj, act, carry16, s0, rp, bsz=b, seq=s, tt=tt, d=d, off_rkv=off_rkv,
                   emit_state=emit_state)
        q, kn, v, kr = _mla_prep(proj, mp, _rope_tables(pos), seq=s, n_heads=n_mla,
                                 off_cq=off_cq, off_ckv=off_ckv, off_kpe=off_kpe)
        return h1, proj, rw, (q, kn, kr, v)

    n_g = d // GROUP_COLS
    pad = META_ROWS - N_META
    meta_h = jnp.concatenate([jnp.zeros((pad, d), F32), meta_tokens.astype(F32)], axis=0)
    meta_pos = jnp.maximum(jnp.arange(META_ROWS) - pad, 0)
    _, proj_m, (_, s_meta), (_, kn_m, kr_m, v_m) = front(
        meta_h, b=1, s=META_ROWS, pos=meta_pos,
        carry16=jnp.zeros((16, n_proj), BF16), s0=jnp.zeros((n_g, GROUP_COLS, GROUP_COLS), F32),
        emit_state=True)
    kn_m = kn_m[pad:].reshape(N_META, n_mla, NOPE_DIM).transpose(1, 0, 2)
    kr_m = jnp.broadcast_to(kr_m[pad:][None], (n_mla, N_META, LANES))
    k_meta = jnp.pad(jnp.concatenate([kn_m, kr_m], axis=2), ((0, 0), (0, LANES - N_META), (0, 0)))
    vt_meta = jnp.pad(v_m[pad:].reshape(N_META, n_mla, V_DIM).transpose(1, 2, 0),
                      ((0, 0), (0, 0), (0, LANES - N_META)))

    h0 = x.reshape(bsz * seq, d)
    h1, proj, (ya,), (q, kn, kr, v) = front(
        h0, b=bsz, s=seq, pos=N_META + jnp.arange(seq),
        carry16=proj_m[META_ROWS - 16:], s0=s_meta, emit_state=False)
    yb = _attn(q, kn, kr, v, k_meta, vt_meta, bsz=bsz, seq=seq, n_heads=n_mla)
    h2 = _outproj(h1, ya, yb, proj, w_o, off_gate=off_gate)
    y = _ffn(h2, ffn2_norm, *f2, final_norm[None, :])
    return y.reshape(bsz, seq, d)
```

```python
import functools

import numpy as np
import jax
import jax.numpy as jnp
from jax import lax
from jax.experimental import pallas as pl
from jax.experimental.pallas import tpu as pltpu

F32 = jnp.float32
BF16 = jnp.bfloat16

N_META = 16
NORM_EPS = 1e-6
RWKV_HEAD = 64
GN_EPS = RWKV_HEAD * 1e-5
NOPE_DIM = 128
ROPE_DIM = 64
V_DIM = 128
QK_DIM = NOPE_DIM + ROPE_DIM
ROPE_THETA = 10000.0

LANES = 128
GROUP_HEADS = 4
GROUP_COLS = GROUP_HEADS * RWKV_HEAD
CHUNK = 64
INV_BASE = 8
RWKV_TILE = 512
ATTN_TILE = 256
LORA_PAD = 128
QHEAD_COLS = 2 * LANES
META_ROWS = 64
VMEM_LIMIT = 56 * 1024 * 1024


def _cparams(sem):
    return pltpu.CompilerParams(dimension_semantics=sem, vmem_limit_bytes=VMEM_LIMIT)


def _sigmoid(x):
    return 0.5 * jnp.tanh(0.5 * x) + 0.5


def _dot(a, b):
    return jnp.dot(a, b, preferred_element_type=F32)


def _dot_nt(a, b):
    return lax.dot_general(a, b, (((1,), (1,)), ((), ())), preferred_element_type=F32)


def _dot_tn(a, b):
    return lax.dot_general(a, b, (((0,), (0,)), ((), ())), preferred_element_type=F32)


def _rms(x, g):
    ms = jnp.mean(x * x, axis=-1, keepdims=True)
    return x * lax.rsqrt(ms + NORM_EPS) * g


def _cast_kernel(w_ref, o_ref):
    o_ref[...] = w_ref[...].astype(o_ref.dtype)


def _to_bf16(w):
    k, n = w.shape
    tk = min(256, k)
    blk = pl.BlockSpec((tk, n), lambda i: (i, 0))
    return pl.pallas_call(
        _cast_kernel,
        grid=(k // tk,),
        in_specs=[blk],
        out_specs=blk,
        out_shape=jax.ShapeDtypeStruct((k, n), BF16),
        compiler_params=_cparams(("parallel",)),
        name="cast",
    )(w)


def _ffn_kernel(x_hbm, g_ref, wg_ref, wu_ref, wd_ref, *rest, final_norm):
    if final_norm:
        g2_ref, o_ref, xbuf, xn_scr, sem = rest
    else:
        o_ref, xbuf, xn_scr, sem = rest
    i, j = pl.program_id(0), pl.program_id(1)
    n_i, n_j = pl.num_programs(0), pl.num_programs(1)
    tm = xbuf.shape[0]

    def x_copy(tile):
        rows = pl.ds(pl.multiple_of(tile * tm, tm), tm)
        return pltpu.make_async_copy(x_hbm.at[rows, :], xbuf, sem)

    @pl.when(jnp.logical_and(i == 0, j == 0))
    def _():
        x_copy(0).start()

    def partial_sum():
        xn = xn_scr[...]
        a = _dot(xn, wg_ref[...])
        b = _dot(xn, wu_ref[...])
        mid = (a * _sigmoid(a) * b).astype(BF16)
        return _dot(mid, wd_ref[...])

    @pl.when(j == 0)
    def _():
        x_copy(i).wait()
        xn_scr[...] = _rms(xbuf[...], g_ref[...]).astype(BF16)
        o_ref[...] = 2.0 * xbuf[...] + partial_sum()

    @pl.when(j > 0)
    def _():
        o_ref[...] += partial_sum()

    @pl.when(jnp.logical_and(j == 1, i + 1 < n_i))
    def _():
        x_copy(i + 1).start()

    @pl.when(j == n_j - 1)
    def _():
        h = 0.5 * o_ref[...]
        o_ref[...] = _rms(h, g2_ref[...]) if final_norm else h


def _ffn(x, g, wg, wu, wd, g2=None):
    m, d = x.shape
    dff = wg.shape[1]
    tm = min(1024, m)
    tf = 512
    assert dff // tf >= 2
    row = pl.BlockSpec((tm, d), lambda i, j: (i, 0))
    vec = pl.BlockSpec((1, d), lambda i, j: (0, 0))
    args = [x, g, wg, wu, wd] + ([g2] if g2 is not None else [])
    return pl.pallas_call(
        functools.partial(_ffn_kernel, final_norm=g2 is not None),
        grid=(m // tm, dff // tf),
        in_specs=[pl.BlockSpec(memory_space=pl.ANY), vec,
                  pl.BlockSpec((d, tf), lambda i, j: (0, j)),
                  pl.BlockSpec((d, tf), lambda i, j: (0, j)),
                  pl.BlockSpec((tf, d), lambda i, j: (j, 0))] + ([vec] if g2 is not None else []),
        out_specs=row,
        out_shape=jax.ShapeDtypeStruct((m, d), F32),
        scratch_shapes=[pltpu.VMEM((tm, d), F32), pltpu.VMEM((tm, d), BF16),
                        pltpu.SemaphoreType.DMA(())],
        compiler_params=_cparams(("arbitrary", "arbitrary")),
        name="ffn",
    )(*args)


def _inproj_kernel(h_ref, g_ref, w_ref, o_ref, xn_scr):
    j = pl.program_id(1)
    tm = h_ref.shape[0]

    @pl.when(j == 0)
    def _():
        step = min(256, tm)
        for r0 in range(0, tm, step):
            rows = slice(r0, r0 + step)
            xn = _rms(h_ref[rows, :], g_ref[...]).astype(BF16)
            xn_scr[rows, :] = xn
            o_ref[rows, :] = _dot(xn, w_ref[...]).astype(o_ref.dtype)

    @pl.when(j > 0)
    def _():
        o_ref[...] = _dot(xn_scr[...], w_ref[...]).astype(o_ref.dtype)


def _inproj(h, g, w, out_dtype):
    m, k = h.shape
    n = w.shape[1]
    tm = min(1024, m)
    tn = 2048 if n % 2048 == 0 else 1024
    return pl.pallas_call(
        _inproj_kernel,
        grid=(m // tm, n // tn),
        in_specs=[pl.BlockSpec((tm, k), lambda i, j: (i, 0)),
                  pl.BlockSpec((1, k), lambda i, j: (0, 0)),
                  pl.BlockSpec((k, tn), lambda i, j: (0, j))],
        out_specs=pl.BlockSpec((tm, tn), lambda i, j: (i, j)),
        out_shape=jax.ShapeDtypeStruct((m, n), out_dtype),
        scratch_shapes=[pltpu.VMEM((tm, k), BF16)],
        compiler_params=_cparams(("parallel", "arbitrary")),
        name="inproj",
    )(h, g, w)


def _iota(shape, dim):
    return lax.broadcasted_iota(jnp.int32, shape, dim)


def _split(x, parts):
    out = []
    for _ in range(parts - 1):
        hi = x.astype(BF16)
        out.append(hi)
        x = x - hi.astype(F32)
    out.append(x.astype(BF16))
    return out


def _dot01(m01, x, parts):
    return sum(_dot(m01, xp) for xp in _split(x, parts))


def _wkv_masks(L):
    C, H = GROUP_COLS, GROUP_HEADS
    hl = H * L
    r_hl, c_c = np.arange(hl)[:, None], np.arange(C)[None, :]
    t, s = np.arange(L)[:, None], np.arange(hl)[None, :] % L
    assert hl == C
    seg = np.arange(C)[:, None] // RWKV_HEAD == c_c // RWKV_HEAD
    sq = np.stack([r_hl // L == c_c // RWKV_HEAD, r_hl // L == np.arange(hl)[None, :] // L, seg])
    inv = [t // INV_BASE == s // INV_BASE]
    m = INV_BASE
    while m < L:
        inv.append((t // (2 * m) == s // (2 * m)) & ((t // m) % 2 == 1) & ((s // m) % 2 == 0))
        m *= 2
    return (jnp.asarray(sq, BF16),
            jnp.asarray(seg, F32),
            jnp.asarray(np.stack([t > s, t >= s, t == s] + inv), F32),
            jnp.asarray(np.arange(L)[:, None] >= np.arange(L)[None, :], BF16))


def _mask_refs(sq_ref, cc_ref, tri3_ref, tri_ref):
    return dict(bd_lc=sq_ref.at[0], bd_ll=sq_ref.at[1], seg=sq_ref.at[2], bd_cc=cc_ref,
                strict=tri3_ref.at[0], incl=tri3_ref.at[1], eye=tri3_ref.at[2], tri=tri_ref,
                inv=[tri3_ref.at[i] for i in range(3, tri3_ref.shape[0])])


N_MASKS = 4
_VEC_ROWS = ("mu_r", "mu_k", "mu_v", "w0", "a0", "k_k", "k_a", "r_k", "gn_w", "gn_b")
_TAIL_NAMES = ("wl", "bonus", "gate")
_SLOT1_NAMES = ("ar", "bk", "vbd", "abd", "bkh", "v") + _TAIL_NAMES
_SLOT2_NAMES = ("q", "y0", "g", "n") + _TAIL_NAMES


def _slot_shapes(tt, L):
    C, n_c, hl = GROUP_COLS, tt // L, GROUP_HEADS * L
    return dict(ar=((n_c, 2 * L, C), BF16), bk=((n_c, 2 * hl, C), BF16), vbd=((n_c, hl, C), BF16),
                abd=((n_c, hl, C), BF16), bkh=((n_c, 2 * L, C), BF16), v=((tt, C), BF16),
                wl=((n_c, 1, C), F32), bonus=((tt, C), F32), gate=((tt, C), F32),
                q=((n_c, L, C), BF16), y0=((n_c, L, C), F32), g=((n_c, C, C), BF16),
                n=((n_c, C, C), F32))


def _shift_rows(x, carry_row):
    prev = pltpu.roll(x, 1, axis=0)
    return jnp.where(_iota(x.shape, 0) == 0, carry_row, prev)


def _bf(x):
    return x.astype(BF16)


def _cat(axis, *xs):
    return jnp.concatenate(xs, axis=axis)


def _tile_rows(x):
    return jnp.concatenate([x] * GROUP_HEADS, axis=0)


def _each(f, *lists):
    return [f(*xs) for xs in zip(*lists)]


def _lora_act_kernel(plo_ref, clo_ref, mulo_ref, act_ref):
    x = plo_ref[...].astype(F32)
    prev = _shift_rows(x, clo_ref[15:16, :].astype(F32))
    lo = x + mulo_ref[...] * (prev - x)
    act_ref[:, :LORA_PAD] = _bf(jnp.tanh(lo[:, :LORA_PAD]))
    act_ref[:, LORA_PAD:2 * LORA_PAD] = _bf(lo[:, LORA_PAD:2 * LORA_PAD])
    act_ref[:, 2 * LORA_PAD:] = _bf(_sigmoid(lo[:, 2 * LORA_PAD:]))


def _lora_act(proj, carry16, mu_lo, *, bsz, seq, off_lo):
    lo_w = 4 * LORA_PAD
    lb = off_lo // lo_w
    return pl.pallas_call(
        _lora_act_kernel,
        grid=(bsz,),
        in_specs=[pl.BlockSpec((seq, lo_w), lambda b: (b, lb)),
                  pl.BlockSpec((16, lo_w), lambda b: (0, lb)),
                  pl.BlockSpec((1, lo_w), lambda b: (0, 0))],
        out_specs=pl.BlockSpec((seq, lo_w), lambda b: (b, 0)),
        out_shape=jax.ShapeDtypeStruct((bsz * seq, lo_w), BF16),
        compiler_params=_cparams(("parallel",)),
        name="lora_act",
    )(proj, carry16, mu_lo)


def _rwkv_prep(tp, gp, L, mk, refs, wr):
    p_refs, c0_refs = refs[0:3], refs[3:6]
    act_ref, vec_ref, wlo_ref, carry_scr = refs[6:]
    tt = act_ref.shape[0]
    vec = lambda name: vec_ref[_VEC_ROWS.index(name):_VEC_ROWS.index(name) + 1, :]

    def shifted(j, mu):
        x = p_refs[j][...].astype(F32)
        carry = jnp.where(tp == 0, c0_refs[j][...], carry_scr[3 * gp + j])
        prev = _shift_rows(x, carry[15:16, :].astype(F32))
        carry_scr[3 * gp + j] = p_refs[j][tt - 16:tt, :]
        return x + mu * (prev - x)

    r = shifted(0, vec("mu_r"))
    k = shifted(1, vec("mu_k"))
    v = shifted(2, vec("mu_v"))
    kkr = k * vec("k_k")
    yield

    u = vec("w0") + _dot(act_ref[:, :LORA_PAD], wlo_ref[:LORA_PAD, :])
    lw = -(np.exp(-0.5) * np.log2(np.e)) * _sigmoid(u)
    alpha = _sigmoid(vec("a0") + _dot(act_ref[:, LORA_PAD:2 * LORA_PAD],
                                      wlo_ref[LORA_PAD:2 * LORA_PAD, :]))
    wr["gate"][...] = _dot(act_ref[:, 2 * LORA_PAD:], wlo_ref[2 * LORA_PAD:, :])
    kk = kkr * lax.rsqrt(jnp.maximum(_dot(_bf(kkr * kkr), mk["seg"][...]), 1e-24))
    k2 = k * (1.0 + (alpha - 1.0) * vec("k_a"))
    b = kk * alpha
    yield

    tri = mk["tri"][...]
    cum = _cat(0, *[_dot01(tri, lw[c * L:(c + 1) * L], 2) for c in range(tt // L)])
    wr["bonus"][...] = _dot(_bf(r * k2 * vec("r_k")), mk["seg"][...]) * v
    e_n = jnp.exp2(-cum)
    a_t = _bf(-kk * jnp.exp2(cum - lw))
    r_t = _bf(r * jnp.exp2(cum))
    b_t = _bf(b * e_n)
    k_t = _bf(k2 * e_n)
    vb = _bf(v)
    wr["v"][...] = vb
    bd_lc = mk["bd_lc"][...]
    for c in range(tt // L):
        sl = slice(c * L, (c + 1) * L)
        w_l = jnp.exp2(cum[sl.stop - 1:sl.stop, :])
        e_l = e_n[sl] * w_l
        wr["wl"][c] = w_l
        wr["ar"][c, :L] = a_t[sl]
        wr["ar"][c, L:] = r_t[sl]
        wr["bk"][c, :GROUP_HEADS * L] = _tile_rows(b_t[sl]) * bd_lc
        wr["bk"][c, GROUP_HEADS * L:] = _tile_rows(k_t[sl]) * bd_lc
        wr["vbd"][c] = _tile_rows(vb[sl]) * bd_lc
        wr["abd"][c] = _tile_rows(a_t[sl]) * bd_lc
        wr["bkh"][c, :L] = _bf(b[sl] * e_l)
        wr["bkh"][c, L:] = _bf(k2[sl] * e_l)


def _rwkv_local(L, mk, rd, wr):
    tt = rd["v"].shape[0]
    n_c = tt // L
    n_h = GROUP_HEADS * L
    C = GROUP_COLS
    chunks = [slice(c * L, (c + 1) * L) for c in range(n_c)]
    bd_lc, bd_ll = mk["bd_lc"][...], mk["bd_ll"][...]
    strict, incl = mk["strict"][...] > 0, mk["incl"][...] > 0
    bd_cc = mk["bd_cc"][...]
    bd = lambda xb: _tile_rows(xb) * bd_lc
    bd_sq = lambda pb: _tile_rows(pb) * bd_ll
    for k in _TAIL_NAMES:
        wr[k][...] = rd[k][...]

    ar = [rd["ar"][c] for c in range(n_c)]
    aa = _each(_dot_nt, ar, [rd["bk"][c] for c in range(n_c)])
    yield
    a_ab = _each(lambda x: jnp.where(strict, x[:L, :n_h], 0.0), aa)
    a_ak = _each(lambda x: _bf(jnp.where(strict, x[:L, n_h:], 0.0)), aa)
    a_rb = _each(lambda x: _bf(jnp.where(incl, x[L:, :n_h], 0.0)), aa)
    a_rk = _each(lambda x: _bf(jnp.where(incl, x[L:, n_h:], 0.0)), aa)

    eye = mk["eye"][...]
    d_base = mk["inv"][0][...]
    p = _each(lambda x: x * d_base, a_ab)
    tp = _each(lambda x: eye + x, p)
    for i in range(int(np.log2(INV_BASE)) - 1):
        if i == 0:
            p = _each(lambda x: _dot(_bf(x), bd_sq(_bf(x))), p)
        else:
            out = _each(lambda x, t: _dot(_bf(_cat(0, x, t)), bd_sq(_bf(x))), p, tp)
            p = _each(lambda o: o[:L], out)
            tp = _each(lambda t, o: t + o[L:], tp, out)
        yield
    tp = _each(lambda t, x: t + _dot(_bf(t), bd_sq(_bf(x))), tp, p)
    yield
    for off_ref in mk["inv"][1:]:
        off = off_ref[...]
        y = _each(lambda x, t: _dot(_bf(x * off), bd_sq(_bf(t))), a_ab, tp)
        yield
        tp = _each(lambda t, x: t + _dot(_bf(t), bd_sq(_bf(x))), tp, y)
        yield
    tb = _each(_bf, tp)

    v_bd = [rd["vbd"][c] for c in range(n_c)]
    x0 = _each(_dot, a_ak, v_bd)
    yield
    ut = _each(lambda t, x, c: _dot(t, _cat(1, bd(_bf(x)), rd["abd"][c])), tb, x0, range(n_c))
    yield
    u0 = _each(lambda x: _bf(x[:, :C]), ut)
    ta = _each(lambda x: _bf(x[:, C:]), ut)
    for c in range(n_c):
        wr["q"][c] = _bf(ar[c][L:].astype(F32) + _dot(a_rb[c], bd(ta[c])))
    yield
    for c in range(n_c):
        wr["y0"][c] = _dot(_cat(1, a_rb[c], a_rk[c]), _cat(0, bd(u0[c]), v_bd[c]))
    yield
    for c in range(n_c):
        bkh = rd["bkh"][c]
        wr["g"][c] = _bf(_dot_tn(ta[c], bkh[:L]) * bd_cc)
        wr["n"][c] = _dot_tn(_cat(0, u0[c], rd["v"][chunks[c]]), bkh) * bd_cc


def _rwkv_serial(tc, gc, L, mk, refs, rd, emit_state):
    s0_ref, vec_ref, y_ref, sout_ref, s_scr = refs
    gnw = vec_ref[_VEC_ROWS.index("gn_w"):_VEC_ROWS.index("gn_w") + 1, :]
    gnb = vec_ref[_VEC_ROWS.index("gn_b"):_VEC_ROWS.index("gn_b") + 1, :]
    n_c = rd["q"].shape[0]
    S = jnp.where(tc == 0, s0_ref[gc], s_scr[gc])
    ys = []
    for c in range(n_c):
        sb = _bf(S)
        ys.append(rd["y0"][c] + _dot_nt(rd["q"][c], sb))
        S = S * rd["wl"][c] + _dot(sb, rd["g"][c]) + rd["n"][c]
        yield
    s_scr[gc] = S
    if emit_state:
        sout_ref[0] = S
    y = ys[0] if n_c == 1 else _cat(0, *ys)

    inv_n = 1.0 / RWKV_HEAD
    seg = mk["seg"][...]
    mean = _dot(_bf(y), seg) * inv_n
    yield
    d = y - mean
    var = _dot(_bf(d * d), seg) * inv_n
    yield
    yn = d * lax.rsqrt(var + GN_EPS) * gnw + gnb
    y_ref[...] = ((yn + rd["bonus"][...]) * rd["gate"][...]).astype(y_ref.dtype)


def _weave(gens, order):
    for who in order:
        next(gens[who], None)
    for gen in gens.values():
        for _ in gen:
            pass


_WEAVE = "ppp" + "ls" * 12


def _rwkv_kernel(*refs, L, n_t, n_g, emit_state):
    n_prep = 9
    n_in = n_prep + N_MASKS + 2
    in_refs, rest = refs[:n_in], refs[n_in:]
    prep_in, chain_in = in_refs[:n_prep], in_refs[-2:]
    mk = _mask_refs(*in_refs[n_prep:n_prep + N_MASKS])
    if emit_state:
        y_ref, sout_ref = rest[:2]
        scr = rest[2:]
    else:
        y_ref, sout_ref = rest[0], None
        scr = rest[1:]
    s_scr, carry_scr = scr[:2]
    scr = list(scr[2:])
    slots1 = [dict(zip(_SLOT1_NAMES, [scr.pop(0) for _ in _SLOT1_NAMES])) for _ in range(2)]
    slots2 = [dict(zip(_SLOT2_NAMES, [scr.pop(0) for _ in _SLOT2_NAMES])) for _ in range(2)]

    s = pl.program_id(0)
    n = pl.num_programs(0) - 2
    sp = jnp.minimum(s, n - 1)
    sc = jnp.clip(s - 2, 0, n - 1)
    tp, gp = (sp // n_g) % n_t, sp % n_g
    tc, gc = (sc // n_g) % n_t, sc % n_g

    @pl.when(s == 0)
    def _():
        for ref in (tuple(slots1[1].values()) + tuple(slots2[0].values()) + tuple(slots2[1].values())
                    + (s_scr, carry_scr)):
            ref[...] = jnp.zeros_like(ref)

    def step(parity):
        gens = dict(
            p=_rwkv_prep(tp, gp, L, mk, prep_in + (carry_scr,), slots1[parity]),
            l=_rwkv_local(L, mk, slots1[1 - parity], slots2[1 - parity]),
            s=_rwkv_serial(tc, gc, L, mk, chain_in + (y_ref, sout_ref, s_scr), slots2[parity],
                           emit_state))
        _weave(gens, _WEAVE)

    for parity in range(2):
        pl.when(s % 2 == parity)(functools.partial(step, parity))


def _rwkv(proj, act, carry16, s0, rp, *, bsz, seq, tt, d, off_rkv, emit_state):
    C = GROUP_COLS
    n_g = d // C
    L = min(CHUNK, tt)
    n_t = seq // tt
    assert n_t == 1 or not emit_state
    n = bsz * n_t * n_g
    lo_w = 4 * LORA_PAD
    cb = off_rkv // C
    masks = _wkv_masks(L)

    def tile(s):
        return s // n_g, s % n_g

    def prep_ix(f):
        return lambda s: f(*tile(jnp.minimum(s, n - 1)))

    def serial_ix(f):
        return lambda s: f(*tile(jnp.clip(s - 2, 0, n - 1)))

    const = lambda a: pl.BlockSpec(a.shape, lambda s, nd=a.ndim: (0,) * nd)
    n_vec = len(_VEC_ROWS)
    rkv = lambda rows: [pl.BlockSpec((rows, C), prep_ix(lambda r, g, j=j: (r if rows == tt else 0,
                                                                        cb + j * n_g + g)))
                        for j in range(3)]
    in_specs = rkv(tt) + rkv(16) + [
        pl.BlockSpec((tt, lo_w), prep_ix(lambda r, g: (r, 0))),
        pl.BlockSpec((n_vec, C), prep_ix(lambda r, g: (0, g))),
        pl.BlockSpec((lo_w, C), prep_ix(lambda r, g: (0, g))),
    ] + [const(m) for m in masks] + [
        const(s0),
        pl.BlockSpec((n_vec, C), serial_ix(lambda r, g: (0, g))),
    ]
    out_specs = [pl.BlockSpec((tt, C), serial_ix(lambda r, g: (r, g)))]
    out_shape = [jax.ShapeDtypeStruct((bsz * seq, d), BF16)]
    if emit_state:
        out_specs.append(pl.BlockSpec((1, C, C), serial_ix(lambda r, g: (r * n_g + g, 0, 0))))
        out_shape.append(jax.ShapeDtypeStruct((bsz * n_g, C, C), F32))
    shp = _slot_shapes(tt, L)
    scratch = [pltpu.VMEM((n_g, C, C), F32), pltpu.VMEM((3 * n_g, 16, C), BF16)]
    scratch += [pltpu.VMEM(*shp[k]) for _ in range(2) for k in _SLOT1_NAMES]
    scratch += [pltpu.VMEM(*shp[k]) for _ in range(2) for k in _SLOT2_NAMES]
    return pl.pallas_call(
        functools.partial(_rwkv_kernel, L=L, n_t=n_t, n_g=n_g, emit_state=emit_state),
        grid=(n + 2,),
        in_specs=in_specs,
        out_specs=out_specs,
        out_shape=out_shape,
        scratch_shapes=scratch,
        compiler_params=_cparams(("arbitrary",)),
        name="rwkv",
    )(proj, proj, proj, carry16, carry16, carry16, act, rp["vec"], rp["w_lora"], *masks, s0,
      rp["vec"])


def _rope(x, c, s1, s2):
    return x * c + pltpu.roll(x, 96, axis=1) * s1 + pltpu.roll(x, 32, axis=1) * s2


def _mla_prep_kernel(cq_ref, ckv_ref, kpe_ref, qn_ref, kvn_ref, wq_ref, wkv_ref,
                     c_ref, s1_ref, s2_ref, q_ref, kn_ref, v_ref, kr_ref, *, n_heads):
    c, s1, s2 = c_ref[...], s1_ref[...], s2_ref[...]
    scale = QK_DIM ** -0.5 * np.log2(np.e)
    cq = _rms(cq_ref[...].astype(F32), qn_ref[...]).astype(BF16)
    for h in range(n_heads):
        qh = _dot(cq, wq_ref[:, h * QHEAD_COLS:(h + 1) * QHEAD_COLS]) * scale
        q_ref[:, h * QHEAD_COLS:h * QHEAD_COLS + LANES] = qh[:, :LANES].astype(BF16)
        q_ref[:, h * QHEAD_COLS + LANES:(h + 1) * QHEAD_COLS] = (
            _rope(qh[:, LANES:], c, s1, s2).astype(BF16))
    ckv = _rms(ckv_ref[...].astype(F32), kvn_ref[...]).astype(BF16)
    hw = n_heads * NOPE_DIM
    kn_ref[...] = _dot(ckv, wkv_ref[:, :hw]).astype(BF16)
    v_ref[...] = _dot(ckv, wkv_ref[:, hw:]).astype(BF16)
    kr_ref[...] = _rope(kpe_ref[...].astype(F32), c, s1, s2).astype(BF16)


def _mla_prep(proj, mp, tabs, *, seq, n_heads, off_cq, off_ckv, off_kpe):
    m = proj.shape[0]
    tm = min(512, seq)
    n_t = seq // tm
    ql = mp["w_uq"].shape[0]
    kl = mp["w_ukv"].shape[0]
    hw = n_heads * NOPE_DIM
    tab = pl.BlockSpec((tm, LANES), lambda i: (i % n_t, 0))
    full = lambda a: pl.BlockSpec(a.shape, lambda i: (0, 0))
    return pl.pallas_call(
        functools.partial(_mla_prep_kernel, n_heads=n_heads),
        grid=(m // tm,),
        in_specs=[pl.BlockSpec((tm, ql), lambda i: (i, off_cq // ql)),
                  pl.BlockSpec((tm, kl), lambda i: (i, off_ckv // kl)),
                  pl.BlockSpec((tm, LANES), lambda i: (i, off_kpe // LANES)),
                  full(mp["q_norm"]), full(mp["kv_norm"]), full(mp["w_uq"]), full(mp["w_ukv"]),
                  tab, tab, tab],
        out_specs=[pl.BlockSpec((tm, n_heads * QHEAD_COLS), lambda i: (i, 0)),
                   pl.BlockSpec((tm, hw), lambda i: (i, 0)),
                   pl.BlockSpec((tm, hw), lambda i: (i, 0)),
                   pl.BlockSpec((tm, LANES), lambda i: (i, 0))],
        out_shape=[jax.ShapeDtypeStruct((m, n_heads * QHEAD_COLS), BF16),
                   jax.ShapeDtypeStruct((m, hw), BF16),
                   jax.ShapeDtypeStruct((m, hw), BF16),
                   jax.ShapeDtypeStruct((m, LANES), BF16)],
        compiler_params=_cparams(("parallel",)),
        name="mla_prep",
    )(proj, proj, proj, mp["q_norm"], mp["kv_norm"], mp["w_uq"], mp["w_ukv"], *tabs)


def _attn_kernel(q_ref, kn_ref, kr_ref, v_ref, km_ref, vmt_ref, o_ref, *, tq):
    seq = q_ref.shape[0]
    n_q = seq // tq
    mp = km_ref.shape[1]
    kx = jnp.concatenate([km_ref[0], jnp.concatenate([kn_ref[...], kr_ref[...]], axis=1)], axis=0)
    vxt = jnp.concatenate([vmt_ref[0], v_ref[...].astype(F32).T.astype(BF16)], axis=1)
    qs = [q_ref[i * tq:(i + 1) * tq, :] for i in range(n_q)]
    n_k = [mp + tq * (i + 1) for i in range(n_q)]
    meta_ok = _iota((mp, tq), 0) < N_META
    causal = _iota((tq, tq), 0) <= _iota((tq, tq), 1)

    ss = [_dot_nt(kx[:n], q) for n, q in zip(n_k, qs)]
    blocks = [[jnp.where(meta_ok, s[:mp], -1e30)] + ([s[mp:n - tq]] if n - tq > mp else [])
              + [jnp.where(causal, s[n - tq:], -1e30)] for n, s in zip(n_k, ss)]
    col_max = lambda x: jnp.max(x, axis=0, keepdims=True)
    ms = [functools.reduce(jnp.maximum, [col_max(x) for x in bl]) for bl in blocks]
    ps = [[jnp.exp2(x - m) for x in bl] for bl, m in zip(blocks, ms)]
    ls = [sum(jnp.sum(x, axis=0, keepdims=True) for x in pl_) for pl_ in ps]
    pv = [_dot(vxt[:, :n], jnp.concatenate([x.astype(BF16) for x in pl_], axis=0))
          for n, pl_ in zip(n_k, ps)]
    for i in range(n_q):
        o_ref[i * tq:(i + 1) * tq, :] = (pv[i] / ls[i]).T.astype(o_ref.dtype)


def _attn(q, kn, kr, v, k_meta, vt_meta, *, bsz, seq, n_heads):
    tq = min(ATTN_TILE, seq)
    mp = k_meta.shape[1]
    return pl.pallas_call(
        functools.partial(_attn_kernel, tq=tq),
        grid=(bsz, n_heads),
        in_specs=[pl.BlockSpec((seq, QHEAD_COLS), lambda b, h: (b, h)),
                  pl.BlockSpec((seq, NOPE_DIM), lambda b, h: (b, h)),
                  pl.BlockSpec((seq, LANES), lambda b, h: (b, 0)),
                  pl.BlockSpec((seq, V_DIM), lambda b, h: (b, h)),
                  pl.BlockSpec((1, mp, QHEAD_COLS), lambda b, h: (h, 0, 0)),
                  pl.BlockSpec((1, V_DIM, mp), lambda b, h: (h, 0, 0))],
        out_specs=pl.BlockSpec((seq, V_DIM), lambda b, h: (b, h)),
        out_shape=jax.ShapeDtypeStruct((bsz * seq, n_heads * V_DIM), BF16),
        compiler_params=_cparams(("parallel", "parallel")),
        name="attn",
    )(q, kn, kr, v, k_meta, vt_meta)


def _outproj_kernel(h_ref, ya_ref, yb_ref, ga_ref, gb_ref, w_ref, o_ref):
    mix = (_sigmoid(ga_ref[...].astype(F32)) * ya_ref[...].astype(F32)
           + _sigmoid(gb_ref[...].astype(F32)) * yb_ref[...].astype(F32))
    o_ref[...] = h_ref[...] + _dot(mix.astype(BF16), w_ref[...])


def _outproj(h, ya, yb, proj, w_out, *, off_gate):
    m, d = h.shape
    tm = min(512, m)
    gb = off_gate // d
    row = pl.BlockSpec((tm, d), lambda i: (i, 0))
    return pl.pallas_call(
        _outproj_kernel,
        grid=(m // tm,),
        in_specs=[row, row, row,
                  pl.BlockSpec((tm, d), lambda i: (i, gb)),
                  pl.BlockSpec((tm, d), lambda i: (i, gb + 1)),
                  pl.BlockSpec((d, d), lambda i: (0, 0))],
        out_specs=row,
        out_shape=jax.ShapeDtypeStruct((m, d), F32),
        compiler_params=_cparams(("parallel",)),
        name="outproj",
    )(h, ya, yb, proj, proj, w_out)


def _pad_cols(w, n):
    return jnp.pad(w, ((0, 0), (0, n - w.shape[1])))


def _pad_rows(w, n):
    return jnp.pad(w, ((0, n - w.shape[0]), (0, 0)))


def _rope_tables(pos):
    inv_freq = (1.0 / (np.float32(ROPE_THETA) ** (np.arange(0, ROPE_DIM, 2, dtype=np.float32)
                                                  / np.float32(ROPE_DIM)))).astype(np.float32)
    ang = pos.astype(np.float32)[:, None] * inv_freq[None, :]
    cos, sin = np.cos(ang), np.sin(ang)
    z = np.zeros_like(cos)
    return tuple(jnp.asarray(np.concatenate(t, axis=1), F32)
                 for t in ([cos, cos, z, z], [-sin, z, z, z], [z, sin, z, z]))


def kernel(x, meta_tokens, ffn1_norm, ffn1_w_gate, ffn1_w_up, ffn1_w_down, mix_norm, w_in,
           tm_mu, w0, w_up, a0, a_up, g_up, k_k, k_a, r_k, gn_w, gn_b, q_norm, w_uq,
           kv_norm, w_ukv, w_out, ffn2_norm, ffn2_w_gate, ffn2_w_up, ffn2_w_down, final_norm):
    bsz, seq, d = x.shape
    assert d % GROUP_COLS == 0 and seq % 16 == 0
    assert ffn1_norm.shape[0] == 1, "single-layer stack"
    n_mla = d // 128
    w_lora, a_lora, g_lora = w_up.shape[1], a_up.shape[1], g_up.shape[1]
    q_lora, kv_lora = w_uq.shape[1], w_ukv.shape[1]
    assert w_lora <= LORA_PAD and a_lora <= LORA_PAD and g_lora == 2 * LORA_PAD
    rwkv_cols = 3 * d + w_lora + a_lora + g_lora
    mla_cols = q_lora + kv_lora + ROPE_DIM

    wi = w_in[0]
    o1, o2 = 3 * d, 3 * d + w_lora
    o3 = o2 + a_lora
    wi_rkv = wi[:, :3 * d]
    wi_lo = jnp.concatenate([_pad_cols(wi[:, o1:o2], LORA_PAD), _pad_cols(wi[:, o2:o3], LORA_PAD),
                             wi[:, o3:rwkv_cols]], axis=1)
    m0 = rwkv_cols
    wi_cq = wi[:, m0:m0 + q_lora]
    wi_ckv = wi[:, m0 + q_lora:m0 + q_lora + kv_lora]
    wi_kpe = _pad_cols(wi[:, m0 + q_lora + kv_lora:m0 + mla_cols], LANES)
    wi_gate = wi[:, m0 + mla_cols:]
    parts = [wi_rkv, wi_gate, wi_lo, wi_cq, wi_ckv, wi_kpe]
    offs = np.cumsum([0] + [p.shape[1] for p in parts])
    off_rkv, off_gate, off_lo, off_cq, off_ckv, off_kpe, n_used = (int(o) for o in offs)
    n_proj = -(-n_used // 1024) * 1024
    parts.append(jnp.zeros((d, n_proj - n_used), wi.dtype))
    w_proj = jnp.concatenate([p.astype(BF16) for p in parts], axis=1)

    mu = tm_mu[0]
    vec = dict(mu_r=mu[None, :d], mu_k=mu[None, d:2 * d], mu_v=mu[None, 2 * d:3 * d], w0=w0, a0=a0,
               k_k=k_k, k_a=k_a, r_k=r_k[0].reshape(1, d), gn_w=gn_w, gn_b=gn_b)
    rp = dict(
        mu_lo=jnp.concatenate([jnp.pad(mu[o1:o2], (0, LORA_PAD - w_lora)),
                               jnp.pad(mu[o2:o3], (0, LORA_PAD - a_lora)), mu[o3:]])[None, :],
        vec=jnp.concatenate([vec[name] for name in _VEC_ROWS], axis=0),
        w_lora=jnp.concatenate([_pad_rows(w_up[0], LORA_PAD), _pad_rows(a_up[0], LORA_PAD),
                                g_up[0]], axis=0).astype(BF16),
    )

    wq = w_uq[0].reshape(q_lora, n_mla, QK_DIM)
    wq = jnp.pad(wq, ((0, 0), (0, 0), (0, QHEAD_COLS - QK_DIM))).reshape(q_lora, n_mla * QHEAD_COLS)
    wkv = w_ukv[0].reshape(kv_lora, n_mla, NOPE_DIM + V_DIM)
    wkv = jnp.concatenate([wkv[:, :, :NOPE_DIM].reshape(kv_lora, -1),
                           wkv[:, :, NOPE_DIM:].reshape(kv_lora, -1)], axis=1)
    mp = dict(q_norm=q_norm, kv_norm=kv_norm, w_uq=wq.astype(BF16), w_ukv=wkv.astype(BF16))

    f1 = (_to_bf16(ffn1_w_gate[0]), _to_bf16(ffn1_w_up[0]), _to_bf16(ffn1_w_down[0]))
    f2 = (_to_bf16(ffn2_w_gate[0]), _to_bf16(ffn2_w_up[0]), _to_bf16(ffn2_w_down[0]))
    w_o = w_out[0].astype(BF16)

    def front(h_rows, *, b, s, pos, carry16, s0, emit_state):
        h1 = _ffn(h_rows, ffn1_norm, *f1)
        proj = _inproj(h1, mix_norm, w_proj, BF16)
        tt = min(RWKV_TILE, s)
        act = _lora_act(proj, carry16, rp["mu_lo"], bsz=b, seq=s, off_lo=off_lo)
        rw = _rwkv(proj, act, carry16, s0, rp, bsz=b, seq=s, tt=tt, d=d, off_rkv=off_rkv,
                   emit_state=emit_state)
        q, kn, v, kr = _mla_prep(proj, mp, _rope_tables(pos), seq=s, n_heads=n_mla,
                                 off_cq=off_cq, off_ckv=off_ckv, off_kpe=off_kpe)
        return h1, proj, rw, (q, kn, kr, v)

    n_g = d // GROUP_COLS
    pad = META_ROWS - N_META
    meta_h = jnp.concatenate([jnp.zeros((pad, d), F32), meta_tokens.astype(F32)], axis=0)
    meta_pos = np.maximum(np.arange(META_ROWS) - pad, 0)
    _, proj_m, (_, s_meta), (_, kn_m, kr_m, v_m) = front(
        meta_h, b=1, s=META_ROWS, pos=meta_pos,
        carry16=jnp.zeros((16, n_proj), BF16), s0=jnp.zeros((n_g, GROUP_COLS, GROUP_COLS), F32),
        emit_state=True)
    kn_m = kn_m[pad:].reshape(N_META, n_mla, NOPE_DIM).transpose(1, 0, 2)
    kr_m = jnp.broadcast_to(kr_m[pad:][None], (n_mla, N_META, LANES))
    k_meta = jnp.pad(jnp.concatenate([kn_m, kr_m], axis=2), ((0, 0), (0, LANES - N_META), (0, 0)))
    vt_meta = jnp.pad(v_m[pad:].reshape(N_META, n_mla, V_DIM).transpose(1, 2, 0),
                      ((0, 0), (0, 0), (0, LANES - N_META)))

    h0 = x.reshape(bsz * seq, d)
    h1, proj, (ya,), (q, kn, kr, v) = front(
        h0, b=bsz, s=seq, pos=N_META + np.arange(seq),
        carry16=proj_m[META_ROWS - 16:], s0=s_meta, emit_state=False)
    yb = _attn(q, kn, kr, v, k_meta, vt_meta, bsz=bsz, seq=seq, n_heads=n_mla)
    h2 = _outproj(h1, ya, yb, proj, w_o, off_gate=off_gate)
    y = _ffn(h2, ffn2_norm, *f2, final_norm[None, :])
    return y.reshape(bsz, seq, d)
```

```python
import functools

import numpy as np
import jax
import jax.numpy as jnp
from jax import lax
from jax.experimental import pallas as pl
from jax.experimental.pallas import tpu as pltpu

F32 = jnp.float32
BF16 = jnp.bfloat16

N_META = 16
NORM_EPS = 1e-6
RWKV_HEAD = 64
GN_EPS = RWKV_HEAD * 1e-5
NOPE_DIM = 128
ROPE_DIM = 64
V_DIM = 128
QK_DIM = NOPE_DIM + ROPE_DIM
ROPE_THETA = 10000.0

LANES = 128
GROUP_HEADS = 4
GROUP_COLS = GROUP_HEADS * RWKV_HEAD
CHUNK = 64
INV_BASE = 8
RWKV_TILE = 512
ATTN_TILE = 256
LORA_PAD = 128
QHEAD_COLS = 2 * LANES
META_ROWS = 64
VMEM_LIMIT = 56 * 1024 * 1024


def _cparams(sem):
    return pltpu.CompilerParams(dimension_semantics=sem, vmem_limit_bytes=VMEM_LIMIT)


def _sigmoid(x):
    return 0.5 * jnp.tanh(0.5 * x) + 0.5


def _dot(a, b):
    return jnp.dot(a, b, preferred_element_type=F32)


def _dot_nt(a, b):
    return lax.dot_general(a, b, (((1,), (1,)), ((), ())), preferred_element_type=F32)


def _dot_tn(a, b):
    return lax.dot_general(a, b, (((0,), (0,)), ((), ())), preferred_element_type=F32)


def _rms(x, g):
    ms = jnp.mean(x * x, axis=-1, keepdims=True)
    return x * lax.rsqrt(ms + NORM_EPS) * g


def _cast_kernel(w_ref, o_ref):
    o_ref[...] = w_ref[...].astype(o_ref.dtype)


def _to_bf16(w):
    k, n = w.shape
    tk = min(256, k)
    blk = pl.BlockSpec((tk, n), lambda i: (i, 0))
    return pl.pallas_call(
        _cast_kernel,
        grid=(k // tk,),
        in_specs=[blk],
        out_specs=blk,
        out_shape=jax.ShapeDtypeStruct((k, n), BF16),
        compiler_params=_cparams(("parallel",)),
        name="cast",
    )(w)


def _ffn_kernel(x_hbm, g_ref, wg_ref, wu_ref, wd_ref, *rest, final_norm):
    if final_norm:
        g2_ref, o_ref, xbuf, xn_scr, sem = rest
    else:
        o_ref, xbuf, xn_scr, sem = rest
    i, j = pl.program_id(0), pl.program_id(1)
    n_i, n_j = pl.num_programs(0), pl.num_programs(1)
    tm = xbuf.shape[0]

    def x_copy(tile):
        rows = pl.ds(pl.multiple_of(tile * tm, tm), tm)
        return pltpu.make_async_copy(x_hbm.at[rows, :], xbuf, sem)

    @pl.when(jnp.logical_and(i == 0, j == 0))
    def _():
        x_copy(0).start()

    def partial_sum():
        xn = xn_scr[...]
        a = _dot(xn, wg_ref[...])
        b = _dot(xn, wu_ref[...])
        mid = (a * _sigmoid(a) * b).astype(BF16)
        return _dot(mid, wd_ref[...])

    @pl.when(j == 0)
    def _():
        x_copy(i).wait()
        xn_scr[...] = _rms(xbuf[...], g_ref[...]).astype(BF16)
        o_ref[...] = 2.0 * xbuf[...] + partial_sum()

    @pl.when(j > 0)
    def _():
        o_ref[...] += partial_sum()

    @pl.when(jnp.logical_and(j == 1, i + 1 < n_i))
    def _():
        x_copy(i + 1).start()

    @pl.when(j == n_j - 1)
    def _():
        h = 0.5 * o_ref[...]
        o_ref[...] = _rms(h, g2_ref[...]) if final_norm else h


def _ffn(x, g, wg, wu, wd, g2=None):
    m, d = x.shape
    dff = wg.shape[1]
    tm = min(1024, m)
    tf = 512
    assert dff // tf >= 2
    row = pl.BlockSpec((tm, d), lambda i, j: (i, 0))
    vec = pl.BlockSpec((1, d), lambda i, j: (0, 0))
    args = [x, g, wg, wu, wd] + ([g2] if g2 is not None else [])
    return pl.pallas_call(
        functools.partial(_ffn_kernel, final_norm=g2 is not None),
        grid=(m // tm, dff // tf),
        in_specs=[pl.BlockSpec(memory_space=pl.ANY), vec,
                  pl.BlockSpec((d, tf), lambda i, j: (0, j)),
                  pl.BlockSpec((d, tf), lambda i, j: (0, j)),
                  pl.BlockSpec((tf, d), lambda i, j: (j, 0))] + ([vec] if g2 is not None else []),
        out_specs=row,
        out_shape=jax.ShapeDtypeStruct((m, d), F32),
        scratch_shapes=[pltpu.VMEM((tm, d), F32), pltpu.VMEM((tm, d), BF16),
                        pltpu.SemaphoreType.DMA(())],
        compiler_params=_cparams(("arbitrary", "arbitrary")),
        name="ffn",
    )(*args)


def _inproj_kernel(h_ref, g_ref, w_ref, o_ref, xn_scr):
    j = pl.program_id(1)
    tm = h_ref.shape[0]

    @pl.when(j == 0)
    def _():
        step = min(256, tm)
        for r0 in range(0, tm, step):
            rows = slice(r0, r0 + step)
            xn = _rms(h_ref[rows, :], g_ref[...]).astype(BF16)
            xn_scr[rows, :] = xn
            o_ref[rows, :] = _dot(xn, w_ref[...]).astype(o_ref.dtype)

    @pl.when(j > 0)
    def _():
        o_ref[...] = _dot(xn_scr[...], w_ref[...]).astype(o_ref.dtype)


def _inproj(h, g, w, out_dtype):
    m, k = h.shape
    n = w.shape[1]
    tm = min(1024, m)
    tn = 2048 if n % 2048 == 0 else 1024
    return pl.pallas_call(
        _inproj_kernel,
        grid=(m // tm, n // tn),
        in_specs=[pl.BlockSpec((tm, k), lambda i, j: (i, 0)),
                  pl.BlockSpec((1, k), lambda i, j: (0, 0)),
                  pl.BlockSpec((k, tn), lambda i, j: (0, j))],
        out_specs=pl.BlockSpec((tm, tn), lambda i, j: (i, j)),
        out_shape=jax.ShapeDtypeStruct((m, n), out_dtype),
        scratch_shapes=[pltpu.VMEM((tm, k), BF16)],
        compiler_params=_cparams(("parallel", "arbitrary")),
        name="inproj",
    )(h, g, w)


def _iota(shape, dim):
    return lax.broadcasted_iota(jnp.int32, shape, dim)


def _split(x, parts):
    out = []
    for _ in range(parts - 1):
        hi = x.astype(BF16)
        out.append(hi)
        x = x - hi.astype(F32)
    out.append(x.astype(BF16))
    return out


def _dot01(m01, x, parts):
    return sum(_dot(m01, xp) for xp in _split(x, parts))


def _wkv_masks(L):
    C, H = GROUP_COLS, GROUP_HEADS
    hl = H * L
    r_hl, c_c = np.arange(hl)[:, None], np.arange(C)[None, :]
    t, s = np.arange(L)[:, None], np.arange(hl)[None, :] % L
    assert hl == C
    seg = np.arange(C)[:, None] // RWKV_HEAD == c_c // RWKV_HEAD
    sq = np.stack([r_hl // L == c_c // RWKV_HEAD, r_hl // L == np.arange(hl)[None, :] // L, seg])
    inv = [(t > s) & (t // INV_BASE == s // INV_BASE)]
    m = INV_BASE
    while m < L:
        inv.append((t // (2 * m) == s // (2 * m)) & ((t // m) % 2 == 1) & ((s // m) % 2 == 0))
        m *= 2
    return (jnp.asarray(sq, BF16),
            jnp.asarray(seg, F32),
            jnp.asarray(np.stack([t > s, t >= s, t == s] + inv), F32),
            jnp.asarray(np.arange(L)[:, None] >= np.arange(L)[None, :], BF16))


def _mask_refs(sq_ref, cc_ref, tri3_ref, tri_ref):
    return dict(bd_lc=sq_ref.at[0], bd_ll=sq_ref.at[1], seg=sq_ref.at[2], bd_cc=cc_ref,
                strict=tri3_ref.at[0], incl=tri3_ref.at[1], eye=tri3_ref.at[2], tri=tri_ref,
                inv=[tri3_ref.at[i] for i in range(3, tri3_ref.shape[0])])


N_MASKS = 4
_VEC_ROWS = ("mu_r", "mu_k", "mu_v", "w0", "a0", "k_k", "k_a", "r_k", "gn_w", "gn_b")
_TAIL_NAMES = ("wl", "bonus", "gate")
_SLOT1_NAMES = ("ar", "bk", "vbd", "abd", "bkh", "v") + _TAIL_NAMES
_SLOT2_NAMES = ("q", "y0", "g", "n") + _TAIL_NAMES


def _slot_shapes(tt, L):
    C, n_c, hl = GROUP_COLS, tt // L, GROUP_HEADS * L
    return dict(ar=((n_c, 2 * L, C), BF16), bk=((n_c, 2 * hl, C), BF16), vbd=((n_c, hl, C), BF16),
                abd=((n_c, hl, C), BF16), bkh=((n_c, 2 * L, C), BF16), v=((tt, C), BF16),
                wl=((n_c, 1, C), F32), bonus=((tt, C), F32), gate=((tt, C), F32),
                q=((n_c, L, C), BF16), y0=((n_c, L, C), F32), g=((n_c, C, C), BF16),
                n=((n_c, C, C), F32))


def _shift_rows(x, carry_row):
    prev = pltpu.roll(x, 1, axis=0)
    return jnp.where(_iota(x.shape, 0) == 0, carry_row, prev)


def _bf(x):
    return x.astype(BF16)


def _cat(axis, *xs):
    return jnp.concatenate(xs, axis=axis)


def _tile_rows(x):
    return jnp.concatenate([x] * GROUP_HEADS, axis=0)


def _each(f, *lists):
    return [f(*xs) for xs in zip(*lists)]


def _lora_act_kernel(plo_ref, clo_ref, mulo_ref, act_ref):
    x = plo_ref[...].astype(F32)
    prev = _shift_rows(x, clo_ref[15:16, :].astype(F32))
    lo = x + mulo_ref[...] * (prev - x)
    act_ref[:, :LORA_PAD] = _bf(jnp.tanh(lo[:, :LORA_PAD]))
    act_ref[:, LORA_PAD:2 * LORA_PAD] = _bf(lo[:, LORA_PAD:2 * LORA_PAD])
    act_ref[:, 2 * LORA_PAD:] = _bf(_sigmoid(lo[:, 2 * LORA_PAD:]))


def _lora_act(proj, carry16, mu_lo, *, bsz, seq, off_lo):
    lo_w = 4 * LORA_PAD
    lb = off_lo // lo_w
    return pl.pallas_call(
        _lora_act_kernel,
        grid=(bsz,),
        in_specs=[pl.BlockSpec((seq, lo_w), lambda b: (b, lb)),
                  pl.BlockSpec((16, lo_w), lambda b: (0, lb)),
                  pl.BlockSpec((1, lo_w), lambda b: (0, 0))],
        out_specs=pl.BlockSpec((seq, lo_w), lambda b: (b, 0)),
        out_shape=jax.ShapeDtypeStruct((bsz * seq, lo_w), BF16),
        compiler_params=_cparams(("parallel",)),
        name="lora_act",
    )(proj, carry16, mu_lo)


def _rwkv_prep(tp, gp, L, mk, refs, wr):
    p_refs, c0_refs = refs[0:3], refs[3:6]
    act_ref, vec_ref, wlo_ref, carry_scr = refs[6:]
    tt = act_ref.shape[0]
    vec = lambda name: vec_ref[_VEC_ROWS.index(name):_VEC_ROWS.index(name) + 1, :]

    def shifted(j, mu):
        x = p_refs[j][...].astype(F32)
        carry = jnp.where(tp == 0, c0_refs[j][...], carry_scr[3 * gp + j])
        prev = _shift_rows(x, carry[15:16, :].astype(F32))
        carry_scr[3 * gp + j] = p_refs[j][tt - 16:tt, :]
        return x + mu * (prev - x)

    r = shifted(0, vec("mu_r"))
    k = shifted(1, vec("mu_k"))
    v = shifted(2, vec("mu_v"))
    kkr = k * vec("k_k")
    yield

    u = vec("w0") + _dot(act_ref[:, :LORA_PAD], wlo_ref[:LORA_PAD, :])
    a_pre = vec("a0") + _dot(act_ref[:, LORA_PAD:2 * LORA_PAD], wlo_ref[LORA_PAD:2 * LORA_PAD, :])
    wr["gate"][...] = _dot(act_ref[:, 2 * LORA_PAD:], wlo_ref[2 * LORA_PAD:, :])
    kk_ss = _dot(_bf(kkr * kkr), mk["seg"][...])
    yield

    n_c = tt // L
    chunks = [slice(c * L, (c + 1) * L) for c in range(n_c)]
    lw = [-(np.exp(-0.5) * np.log2(np.e)) * _sigmoid(u[sl]) for sl in chunks]
    alpha = [_sigmoid(a_pre[sl]) for sl in chunks]
    kk = [kkr[sl] * lax.rsqrt(jnp.maximum(kk_ss[sl], 1e-24)) for sl in chunks]
    k2 = [k[sl] * (1.0 + (al - 1.0) * vec("k_a")) for sl, al in zip(chunks, alpha)]
    tri = mk["tri"][...]
    cum = [_dot01(tri, x, 2) for x in lw]
    rk2 = _cat(0, *[r[sl] * x for sl, x in zip(chunks, k2)]) * vec("r_k")
    wr["bonus"][...] = _dot(_bf(rk2), mk["seg"][...]) * v
    wr["v"][...] = _bf(v)
    yield

    bd_lc = mk["bd_lc"][...]
    for c, sl in enumerate(chunks):
        b = kk[c] * alpha[c]
        e_n = jnp.exp2(-cum[c])
        w_l = jnp.exp2(cum[c][L - 1:L, :])
        e_l = e_n * w_l
        a_t = _bf(-kk[c] * jnp.exp2(cum[c] - lw[c]))
        wr["wl"][c] = w_l
        wr["ar"][c, :L] = a_t
        wr["ar"][c, L:] = _bf(r[sl] * jnp.exp2(cum[c]))
        wr["bk"][c, :GROUP_HEADS * L] = _tile_rows(_bf(b * e_n)) * bd_lc
        wr["bk"][c, GROUP_HEADS * L:] = _tile_rows(_bf(k2[c] * e_n)) * bd_lc
        wr["vbd"][c] = _tile_rows(_bf(v[sl])) * bd_lc
        wr["abd"][c] = _tile_rows(a_t) * bd_lc
        wr["bkh"][c, :L] = _bf(b * e_l)
        wr["bkh"][c, L:] = _bf(k2[c] * e_l)


def _rwkv_local(L, mk, rd, wr):
    tt = rd["v"].shape[0]
    n_c = tt // L
    n_h = GROUP_HEADS * L
    C = GROUP_COLS
    chunks = [slice(c * L, (c + 1) * L) for c in range(n_c)]
    bd_lc, bd_ll = mk["bd_lc"][...], mk["bd_ll"][...]
    strict, incl = mk["strict"][...] > 0, mk["incl"][...] > 0
    bd_cc = mk["bd_cc"][...]
    bd = lambda xb: _tile_rows(xb) * bd_lc
    bd_sq = lambda pb: _tile_rows(pb) * bd_ll
    for k in _TAIL_NAMES:
        wr[k][...] = rd[k][...]

    ar = [rd["ar"][c] for c in range(n_c)]
    aa = _each(_dot_nt, ar, [rd["bk"][c] for c in range(n_c)])
    yield
    a_raw = _each(lambda x: x[:L, :n_h], aa)
    a_ak = _each(lambda x: _bf(jnp.where(strict, x[:L, n_h:], 0.0)), aa)
    a_rb = _each(lambda x: _bf(jnp.where(incl, x[L:, :n_h], 0.0)), aa)
    a_rk = _each(lambda x: _bf(jnp.where(incl, x[L:, n_h:], 0.0)), aa)

    eye = mk["eye"][...]
    d_base = mk["inv"][0][...] > 0
    p = _each(lambda x: jnp.where(d_base, x, 0.0), a_raw)
    tp = _each(lambda x: eye + x, p)
    for i in range(int(np.log2(INV_BASE)) - 1):
        if i == 0:
            p = _each(lambda x: _dot(_bf(x), bd_sq(_bf(x))), p)
        else:
            out = _each(lambda x, t: _dot(_bf(_cat(0, x, t)), bd_sq(_bf(x))), p, tp)
            p = _each(lambda o: o[:L], out)
            tp = _each(lambda t, o: t + o[L:], tp, out)
        yield
    tp = _each(lambda t, x: t + _dot(_bf(t), bd_sq(_bf(x))), tp, p)
    yield
    a_off = [_each(lambda x: bd_sq(_bf(jnp.where(off_ref[...] > 0, x, 0.0))), a_raw)
             for off_ref in mk["inv"][1:]]
    for a_lvl in a_off:
        w = _each(lambda t, a: _dot(_bf(t), a), tp, a_lvl)
        yield
        tp = _each(lambda t, x: t + _dot(_bf(x), bd_sq(_bf(t))), tp, w)
        yield
    tb = _each(_bf, tp)

    v_bd = [rd["vbd"][c] for c in range(n_c)]
    x0 = _each(_dot, a_ak, v_bd)
    yield
    ut = _each(lambda t, x, c: _dot(t, _cat(1, bd(_bf(x)), rd["abd"][c])), tb, x0, range(n_c))
    yield
    u0 = _each(lambda x: _bf(x[:, :C]), ut)
    ta = _each(lambda x: _bf(x[:, C:]), ut)
    for c in range(n_c):
        wr["q"][c] = _bf(ar[c][L:].astype(F32) + _dot(a_rb[c], bd(ta[c])))
    yield
    for c in range(n_c):
        wr["y0"][c] = _dot(_cat(1, a_rb[c], a_rk[c]), _cat(0, bd(u0[c]), v_bd[c]))
    yield
    for c in range(n_c):
        bkh = rd["bkh"][c]
        wr["g"][c] = _bf(_dot_tn(ta[c], bkh[:L]) * bd_cc)
        wr["n"][c] = _dot_tn(_cat(0, u0[c], rd["v"][chunks[c]]), bkh) * bd_cc


def _rwkv_serial(tc, gc, L, mk, refs, rd, emit_state):
    s0_ref, vec_ref, y_ref, sout_ref, s_scr = refs
    gnw = vec_ref[_VEC_ROWS.index("gn_w"):_VEC_ROWS.index("gn_w") + 1, :]
    gnb = vec_ref[_VEC_ROWS.index("gn_b"):_VEC_ROWS.index("gn_b") + 1, :]
    n_c = rd["q"].shape[0]
    S = jnp.where(tc == 0, s0_ref[gc], s_scr[gc])
    ys = []
    for c in range(n_c):
        sb = _bf(S)
        ys.append(rd["y0"][c] + _dot_nt(rd["q"][c], sb))
        S = S * rd["wl"][c] + _dot(sb, rd["g"][c]) + rd["n"][c]
        yield
    s_scr[gc] = S
    if emit_state:
        sout_ref[0] = S
    y = ys[0] if n_c == 1 else _cat(0, *ys)

    inv_n = 1.0 / RWKV_HEAD
    seg = mk["seg"][...]
    mean = _dot(_bf(y), seg) * inv_n
    yield
    d = y - mean
    var = _dot(_bf(d * d), seg) * inv_n
    yield
    yn = d * lax.rsqrt(var + GN_EPS) * gnw + gnb
    y_ref[...] = ((yn + rd["bonus"][...]) * rd["gate"][...]).astype(y_ref.dtype)


def _weave(gens, order):
    for who in order:
        next(gens[who], None)
    for gen in gens.values():
        for _ in gen:
            pass


_WEAVE = "ppp" + "ls" * 12


def _rwkv_kernel(*refs, L, n_t, n_g, emit_state):
    n_prep = 9
    n_in = n_prep + N_MASKS + 2
    in_refs, rest = refs[:n_in], refs[n_in:]
    prep_in, chain_in = in_refs[:n_prep], in_refs[-2:]
    mk = _mask_refs(*in_refs[n_prep:n_prep + N_MASKS])
    if emit_state:
        y_ref, sout_ref = rest[:2]
        scr = rest[2:]
    else:
        y_ref, sout_ref = rest[0], None
        scr = rest[1:]
    s_scr, carry_scr = scr[:2]
    scr = list(scr[2:])
    slots1 = [dict(zip(_SLOT1_NAMES, [scr.pop(0) for _ in _SLOT1_NAMES])) for _ in range(2)]
    slots2 = [dict(zip(_SLOT2_NAMES, [scr.pop(0) for _ in _SLOT2_NAMES])) for _ in range(2)]

    s = pl.program_id(0)
    n = pl.num_programs(0) - 2
    sp = jnp.minimum(s, n - 1)
    sc = jnp.clip(s - 2, 0, n - 1)
    tp, gp = (sp // n_g) % n_t, sp % n_g
    tc, gc = (sc // n_g) % n_t, sc % n_g

    @pl.when(s == 0)
    def _():
        for ref in (tuple(slots1[1].values()) + tuple(slots2[0].values()) + tuple(slots2[1].values())
                    + (s_scr, carry_scr)):
            ref[...] = jnp.zeros_like(ref)

    def step(parity):
        gens = dict(
            p=_rwkv_prep(tp, gp, L, mk, prep_in + (carry_scr,), slots1[parity]),
            l=_rwkv_local(L, mk, slots1[1 - parity], slots2[1 - parity]),
            s=_rwkv_serial(tc, gc, L, mk, chain_in + (y_ref, sout_ref, s_scr), slots2[parity],
                           emit_state))
        _weave(gens, _WEAVE)

    for parity in range(2):
        pl.when(s % 2 == parity)(functools.partial(step, parity))


def _rwkv(proj, act, carry16, s0, rp, *, bsz, seq, tt, d, off_rkv, emit_state):
    C = GROUP_COLS
    n_g = d // C
    L = min(CHUNK, tt)
    n_t = seq // tt
    assert n_t == 1 or not emit_state
    n = bsz * n_t * n_g
    lo_w = 4 * LORA_PAD
    cb = off_rkv // C
    masks = _wkv_masks(L)

    def tile(s):
        return s // n_g, s % n_g

    def prep_ix(f):
        return lambda s: f(*tile(jnp.minimum(s, n - 1)))

    def serial_ix(f):
        return lambda s: f(*tile(jnp.clip(s - 2, 0, n - 1)))

    const = lambda a: pl.BlockSpec(a.shape, lambda s, nd=a.ndim: (0,) * nd)
    n_vec = len(_VEC_ROWS)
    rkv = lambda rows: [pl.BlockSpec((rows, C), prep_ix(lambda r, g, j=j: (r if rows == tt else 0,
                                                                        cb + j * n_g + g)))
                        for j in range(3)]
    in_specs = rkv(tt) + rkv(16) + [
        pl.BlockSpec((tt, lo_w), prep_ix(lambda r, g: (r, 0))),
        pl.BlockSpec((n_vec, C), prep_ix(lambda r, g: (0, g))),
        pl.BlockSpec((lo_w, C), prep_ix(lambda r, g: (0, g))),
    ] + [const(m) for m in masks] + [
        const(s0),
        pl.BlockSpec((n_vec, C), serial_ix(lambda r, g: (0, g))),
    ]
    out_specs = [pl.BlockSpec((tt, C), serial_ix(lambda r, g: (r, g)))]
    out_shape = [jax.ShapeDtypeStruct((bsz * seq, d), BF16)]
    if emit_state:
        out_specs.append(pl.BlockSpec((1, C, C), serial_ix(lambda r, g: (r * n_g + g, 0, 0))))
        out_shape.append(jax.ShapeDtypeStruct((bsz * n_g, C, C), F32))
    shp = _slot_shapes(tt, L)
    scratch = [pltpu.VMEM((n_g, C, C), F32), pltpu.VMEM((3 * n_g, 16, C), BF16)]
    scratch += [pltpu.VMEM(*shp[k]) for _ in range(2) for k in _SLOT1_NAMES]
    scratch += [pltpu.VMEM(*shp[k]) for _ in range(2) for k in _SLOT2_NAMES]
    return pl.pallas_call(
        functools.partial(_rwkv_kernel, L=L, n_t=n_t, n_g=n_g, emit_state=emit_state),
        grid=(n + 2,),
        in_specs=in_specs,
        out_specs=out_specs,
        out_shape=out_shape,
        scratch_shapes=scratch,
        compiler_params=_cparams(("arbitrary",)),
        name="rwkv",
    )(proj, proj, proj, carry16, carry16, carry16, act, rp["vec"], rp["w_lora"], *masks, s0,
      rp["vec"])


def _rope(x, c, s1, s2):
    return x * c + pltpu.roll(x, 96, axis=1) * s1 + pltpu.roll(x, 32, axis=1) * s2


def _mla_prep_kernel(cq_ref, ckv_ref, kpe_ref, qn_ref, kvn_ref, wq_ref, wkv_ref,
                     c_ref, s1_ref, s2_ref, q_ref, kn_ref, v_ref, kr_ref, *, n_heads):
    c, s1, s2 = c_ref[...], s1_ref[...], s2_ref[...]
    scale = QK_DIM ** -0.5 * np.log2(np.e)
    cq = _rms(cq_ref[...].astype(F32), qn_ref[...]).astype(BF16)
    for h in range(n_heads):
        qh = _dot(cq, wq_ref[:, h * QHEAD_COLS:(h + 1) * QHEAD_COLS]) * scale
        q_ref[:, h * QHEAD_COLS:h * QHEAD_COLS + LANES] = qh[:, :LANES].astype(BF16)
        q_ref[:, h * QHEAD_COLS + LANES:(h + 1) * QHEAD_COLS] = (
            _rope(qh[:, LANES:], c, s1, s2).astype(BF16))
    ckv = _rms(ckv_ref[...].astype(F32), kvn_ref[...]).astype(BF16)
    hw = n_heads * NOPE_DIM
    kn_ref[...] = _dot(ckv, wkv_ref[:, :hw]).astype(BF16)
    v_ref[...] = _dot(ckv, wkv_ref[:, hw:]).astype(BF16)
    kr_ref[...] = _rope(kpe_ref[...].astype(F32), c, s1, s2).astype(BF16)


def _mla_prep(proj, mp, tabs, *, seq, n_heads, off_cq, off_ckv, off_kpe):
    m = proj.shape[0]
    tm = min(512, seq)
    n_t = seq // tm
    ql = mp["w_uq"].shape[0]
    kl = mp["w_ukv"].shape[0]
    hw = n_heads * NOPE_DIM
    tab = pl.BlockSpec((tm, LANES), lambda i: (i % n_t, 0))
    full = lambda a: pl.BlockSpec(a.shape, lambda i: (0, 0))
    return pl.pallas_call(
        functools.partial(_mla_prep_kernel, n_heads=n_heads),
        grid=(m // tm,),
        in_specs=[pl.BlockSpec((tm, ql), lambda i: (i, off_cq // ql)),
                  pl.BlockSpec((tm, kl), lambda i: (i, off_ckv // kl)),
                  pl.BlockSpec((tm, LANES), lambda i: (i, off_kpe // LANES)),
                  full(mp["q_norm"]), full(mp["kv_norm"]), full(mp["w_uq"]), full(mp["w_ukv"]),
                  tab, tab, tab],
        out_specs=[pl.BlockSpec((tm, n_heads * QHEAD_COLS), lambda i: (i, 0)),
                   pl.BlockSpec((tm, hw), lambda i: (i, 0)),
                   pl.BlockSpec((tm, hw), lambda i: (i, 0)),
                   pl.BlockSpec((tm, LANES), lambda i: (i, 0))],
        out_shape=[jax.ShapeDtypeStruct((m, n_heads * QHEAD_COLS), BF16),
                   jax.ShapeDtypeStruct((m, hw), BF16),
                   jax.ShapeDtypeStruct((m, hw), BF16),
                   jax.ShapeDtypeStruct((m, LANES), BF16)],
        compiler_params=_cparams(("parallel",)),
        name="mla_prep",
    )(proj, proj, proj, mp["q_norm"], mp["kv_norm"], mp["w_uq"], mp["w_ukv"], *tabs)


def _attn_kernel(q_ref, kn_ref, kr_ref, v_ref, km_ref, vmt_ref, o_ref, *, tq):
    seq = q_ref.shape[0]
    n_q = seq // tq
    mp = km_ref.shape[1]
    kx = jnp.concatenate([km_ref[0], jnp.concatenate([kn_ref[...], kr_ref[...]], axis=1)], axis=0)
    vxt = jnp.concatenate([vmt_ref[0], v_ref[...].astype(F32).T.astype(BF16)], axis=1)
    qs = [q_ref[i * tq:(i + 1) * tq, :] for i in range(n_q)]
    n_k = [mp + tq * (i + 1) for i in range(n_q)]
    meta_ok = _iota((mp, tq), 0) < N_META
    causal = _iota((tq, tq), 0) <= _iota((tq, tq), 1)

    ss = [_dot_nt(kx[:n], q) for n, q in zip(n_k, qs)]
    blocks = [[jnp.where(meta_ok, s[:mp], -1e30)] + ([s[mp:n - tq]] if n - tq > mp else [])
              + [jnp.where(causal, s[n - tq:], -1e30)] for n, s in zip(n_k, ss)]
    col_max = lambda x: jnp.max(x, axis=0, keepdims=True)
    ms = [functools.reduce(jnp.maximum, [col_max(x) for x in bl]) for bl in blocks]
    ps = [[jnp.exp2(x - m) for x in bl] for bl, m in zip(blocks, ms)]
    ls = [sum(jnp.sum(x, axis=0, keepdims=True) for x in pl_) for pl_ in ps]
    pv = [_dot(vxt[:, :n], jnp.concatenate([x.astype(BF16) for x in pl_], axis=0))
          for n, pl_ in zip(n_k, ps)]
    for i in range(n_q):
        o_ref[i * tq:(i + 1) * tq, :] = (pv[i] / ls[i]).T.astype(o_ref.dtype)


def _attn(q, kn, kr, v, k_meta, vt_meta, *, bsz, seq, n_heads):
    tq = min(ATTN_TILE, seq)
    mp = k_meta.shape[1]
    return pl.pallas_call(
        functools.partial(_attn_kernel, tq=tq),
        grid=(bsz, n_heads),
        in_specs=[pl.BlockSpec((seq, QHEAD_COLS), lambda b, h: (b, h)),
                  pl.BlockSpec((seq, NOPE_DIM), lambda b, h: (b, h)),
                  pl.BlockSpec((seq, LANES), lambda b, h: (b, 0)),
                  pl.BlockSpec((seq, V_DIM), lambda b, h: (b, h)),
                  pl.BlockSpec((1, mp, QHEAD_COLS), lambda b, h: (h, 0, 0)),
                  pl.BlockSpec((1, V_DIM, mp), lambda b, h: (h, 0, 0))],
        out_specs=pl.BlockSpec((seq, V_DIM), lambda b, h: (b, h)),
        out_shape=jax.ShapeDtypeStruct((bsz * seq, n_heads * V_DIM), BF16),
        compiler_params=_cparams(("parallel", "parallel")),
        name="attn",
    )(q, kn, kr, v, k_meta, vt_meta)


def _outproj_kernel(h_ref, ya_ref, yb_ref, ga_ref, gb_ref, w_ref, o_ref):
    mix = (_sigmoid(ga_ref[...].astype(F32)) * ya_ref[...].astype(F32)
           + _sigmoid(gb_ref[...].astype(F32)) * yb_ref[...].astype(F32))
    o_ref[...] = h_ref[...] + _dot(mix.astype(BF16), w_ref[...])


def _outproj(h, ya, yb, proj, w_out, *, off_gate):
    m, d = h.shape
    tm = min(512, m)
    gb = off_gate // d
    row = pl.BlockSpec((tm, d), lambda i: (i, 0))
    return pl.pallas_call(
        _outproj_kernel,
        grid=(m // tm,),
        in_specs=[row, row, row,
                  pl.BlockSpec((tm, d), lambda i: (i, gb)),
                  pl.BlockSpec((tm, d), lambda i: (i, gb + 1)),
                  pl.BlockSpec((d, d), lambda i: (0, 0))],
        out_specs=row,
        out_shape=jax.ShapeDtypeStruct((m, d), F32),
        compiler_params=_cparams(("parallel",)),
        name="outproj",
    )(h, ya, yb, proj, proj, w_out)


def _pad_cols(w, n):
    return jnp.pad(w, ((0, 0), (0, n - w.shape[1])))


def _pad_rows(w, n):
    return jnp.pad(w, ((0, n - w.shape[0]), (0, 0)))


def _rope_tables(pos):
    inv_freq = (1.0 / (np.float32(ROPE_THETA) ** (np.arange(0, ROPE_DIM, 2, dtype=np.float32)
                                                  / np.float32(ROPE_DIM)))).astype(np.float32)
    ang = pos.astype(np.float32)[:, None] * inv_freq[None, :]
    cos, sin = np.cos(ang), np.sin(ang)
    z = np.zeros_like(cos)
    return tuple(jnp.asarray(np.concatenate(t, axis=1), F32)
                 for t in ([cos, cos, z, z], [-sin, z, z, z], [z, sin, z, z]))


def kernel(x, meta_tokens, ffn1_norm, ffn1_w_gate, ffn1_w_up, ffn1_w_down, mix_norm, w_in,
           tm_mu, w0, w_up, a0, a_up, g_up, k_k, k_a, r_k, gn_w, gn_b, q_norm, w_uq,
           kv_norm, w_ukv, w_out, ffn2_norm, ffn2_w_gate, ffn2_w_up, ffn2_w_down, final_norm):
    bsz, seq, d = x.shape
    assert d % GROUP_COLS == 0 and seq % 16 == 0
    assert ffn1_norm.shape[0] == 1, "single-layer stack"
    n_mla = d // 128
    w_lora, a_lora, g_lora = w_up.shape[1], a_up.shape[1], g_up.shape[1]
    q_lora, kv_lora = w_uq.shape[1], w_ukv.shape[1]
    assert w_lora <= LORA_PAD and a_lora <= LORA_PAD and g_lora == 2 * LORA_PAD
    rwkv_cols = 3 * d + w_lora + a_lora + g_lora
    mla_cols = q_lora + kv_lora + ROPE_DIM

    wi = w_in[0]
    o1, o2 = 3 * d, 3 * d + w_lora
    o3 = o2 + a_lora
    wi_rkv = wi[:, :3 * d]
    wi_lo = jnp.concatenate([_pad_cols(wi[:, o1:o2], LORA_PAD), _pad_cols(wi[:, o2:o3], LORA_PAD),
                             wi[:, o3:rwkv_cols]], axis=1)
    m0 = rwkv_cols
    wi_cq = wi[:, m0:m0 + q_lora]
    wi_ckv = wi[:, m0 + q_lora:m0 + q_lora + kv_lora]
    wi_kpe = _pad_cols(wi[:, m0 + q_lora + kv_lora:m0 + mla_cols], LANES)
    wi_gate = wi[:, m0 + mla_cols:]
    parts = [wi_rkv, wi_gate, wi_lo, wi_cq, wi_ckv, wi_kpe]
    offs = np.cumsum([0] + [p.shape[1] for p in parts])
    off_rkv, off_gate, off_lo, off_cq, off_ckv, off_kpe, n_used = (int(o) for o in offs)
    n_proj = -(-n_used // 1024) * 1024
    parts.append(jnp.zeros((d, n_proj - n_used), wi.dtype))
    w_proj = jnp.concatenate([p.astype(BF16) for p in parts], axis=1)

    mu = tm_mu[0]
    vec = dict(mu_r=mu[None, :d], mu_k=mu[None, d:2 * d], mu_v=mu[None, 2 * d:3 * d], w0=w0, a0=a0,
               k_k=k_k, k_a=k_a, r_k=r_k[0].reshape(1, d), gn_w=gn_w, gn_b=gn_b)
    rp = dict(
        mu_lo=jnp.concatenate([jnp.pad(mu[o1:o2], (0, LORA_PAD - w_lora)),
                               jnp.pad(mu[o2:o3], (0, LORA_PAD - a_lora)), mu[o3:]])[None, :],
        vec=jnp.concatenate([vec[name] for name in _VEC_ROWS], axis=0),
        w_lora=jnp.concatenate([_pad_rows(w_up[0], LORA_PAD), _pad_rows(a_up[0], LORA_PAD),
                                g_up[0]], axis=0).astype(BF16),
    )

    wq = w_uq[0].reshape(q_lora, n_mla, QK_DIM)
    wq = jnp.pad(wq, ((0, 0), (0, 0), (0, QHEAD_COLS - QK_DIM))).reshape(q_lora, n_mla * QHEAD_COLS)
    wkv = w_ukv[0].reshape(kv_lora, n_mla, NOPE_DIM + V_DIM)
    wkv = jnp.concatenate([wkv[:, :, :NOPE_DIM].reshape(kv_lora, -1),
                           wkv[:, :, NOPE_DIM:].reshape(kv_lora, -1)], axis=1)
    mp = dict(q_norm=q_norm, kv_norm=kv_norm, w_uq=wq.astype(BF16), w_ukv=wkv.astype(BF16))

    f1 = (_to_bf16(ffn1_w_gate[0]), _to_bf16(ffn1_w_up[0]), _to_bf16(ffn1_w_down[0]))
    f2 = (_to_bf16(ffn2_w_gate[0]), _to_bf16(ffn2_w_up[0]), _to_bf16(ffn2_w_down[0]))
    w_o = w_out[0].astype(BF16)

    def front(h_rows, *, b, s, pos, carry16, s0, emit_state):
        h1 = _ffn(h_rows, ffn1_norm, *f1)
        proj = _inproj(h1, mix_norm, w_proj, BF16)
        tt = min(RWKV_TILE, s)
        act = _lora_act(proj, carry16, rp["mu_lo"], bsz=b, seq=s, off_lo=off_lo)
        rw = _rwkv(proj, act, carry16, s0, rp, bsz=b, seq=s, tt=tt, d=d, off_rkv=off_rkv,
                   emit_state=emit_state)
        q, kn, v, kr = _mla_prep(proj, mp, _rope_tables(pos), seq=s, n_heads=n_mla,
                                 off_cq=off_cq, off_ckv=off_ckv, off_kpe=off_kpe)
        return h1, proj, rw, (q, kn, kr, v)

    n_g = d // GROUP_COLS
    pad = META_ROWS - N_META
    meta_h = jnp.concatenate([jnp.zeros((pad, d), F32), meta_tokens.astype(F32)], axis=0)
    meta_pos = np.maximum(np.arange(META_ROWS) - pad, 0)
    _, proj_m, (_, s_meta), (_, kn_m, kr_m, v_m) = front(
        meta_h, b=1, s=META_ROWS, pos=meta_pos,
        carry16=jnp.zeros((16, n_proj), BF16), s0=jnp.zeros((n_g, GROUP_COLS, GROUP_COLS), F32),
        emit_state=True)
    kn_m = kn_m[pad:].reshape(N_META, n_mla, NOPE_DIM).transpose(1, 0, 2)
    kr_m = jnp.broadcast_to(kr_m[pad:][None], (n_mla, N_META, LANES))
    k_meta = jnp.pad(jnp.concatenate([kn_m, kr_m], axis=2), ((0, 0), (0, LANES - N_META), (0, 0)))
    vt_meta = jnp.pad(v_m[pad:].reshape(N_META, n_mla, V_DIM).transpose(1, 2, 0),
                      ((0, 0), (0, 0), (0, LANES - N_META)))

    h0 = x.reshape(bsz * seq, d)
    h1, proj, (ya,), (q, kn, kr, v) = front(
        h0, b=bsz, s=seq, pos=N_META + np.arange(seq),
        carry16=proj_m[META_ROWS - 16:], s0=s_meta, emit_state=False)
    yb = _attn(q, kn, kr, v, k_meta, vt_meta, bsz=bsz, seq=seq, n_heads=n_mla)
    h2 = _outproj(h1, ya, yb, proj, w_o, off_gate=off_gate)
    y = _ffn(h2, ffn2_norm, *f2, final_norm[None, :])
    return y.reshape(bsz, seq, d)
```

```python
import functools

import numpy as np
import jax
import jax.numpy as jnp
from jax import lax
from jax.experimental import pallas as pl
from jax.experimental.pallas import tpu as pltpu

F32 = jnp.float32
BF16 = jnp.bfloat16

N_META = 16
NORM_EPS = 1e-6
RWKV_HEAD = 64
GN_EPS = RWKV_HEAD * 1e-5
NOPE_DIM = 128
ROPE_DIM = 64
V_DIM = 128
QK_DIM = NOPE_DIM + ROPE_DIM
ROPE_THETA = 10000.0

LANES = 128
GROUP_HEADS = 4
GROUP_COLS = GROUP_HEADS * RWKV_HEAD
CHUNK = 64
INV_BASE = 8
RWKV_TILE = 512
ATTN_TILE = 256
LORA_PAD = 128
QHEAD_COLS = 2 * LANES
META_ROWS = 64
VMEM_LIMIT = 56 * 1024 * 1024


def _cparams(sem):
    return pltpu.CompilerParams(dimension_semantics=sem, vmem_limit_bytes=VMEM_LIMIT)


def _sigmoid(x):
    return 0.5 * jnp.tanh(0.5 * x) + 0.5


def _dot(a, b):
    return jnp.dot(a, b, preferred_element_type=F32)


def _dot_nt(a, b):
    return lax.dot_general(a, b, (((1,), (1,)), ((), ())), preferred_element_type=F32)


def _dot_tn(a, b):
    return lax.dot_general(a, b, (((0,), (0,)), ((), ())), preferred_element_type=F32)


def _rms(x, g):
    ms = jnp.mean(x * x, axis=-1, keepdims=True)
    return x * lax.rsqrt(ms + NORM_EPS) * g


def _cast_kernel(w_ref, o_ref):
    o_ref[...] = w_ref[...].astype(o_ref.dtype)


def _to_bf16(w):
    k, n = w.shape
    tk = min(256, k)
    blk = pl.BlockSpec((tk, n), lambda i: (i, 0))
    return pl.pallas_call(
        _cast_kernel,
        grid=(k // tk,),
        in_specs=[blk],
        out_specs=blk,
        out_shape=jax.ShapeDtypeStruct((k, n), BF16),
        compiler_params=_cparams(("parallel",)),
        name="cast",
    )(w)


def _ffn_kernel(x_hbm, g_ref, wg_ref, wu_ref, wd_ref, *rest, final_norm):
    if final_norm:
        g2_ref, o_ref, xbuf, xn_scr, sem = rest
    else:
        o_ref, xbuf, xn_scr, sem = rest
    i, j = pl.program_id(0), pl.program_id(1)
    n_i, n_j = pl.num_programs(0), pl.num_programs(1)
    tm = xbuf.shape[0]

    def x_copy(tile):
        rows = pl.ds(pl.multiple_of(tile * tm, tm), tm)
        return pltpu.make_async_copy(x_hbm.at[rows, :], xbuf, sem)

    @pl.when(jnp.logical_and(i == 0, j == 0))
    def _():
        x_copy(0).start()

    def partial_sum():
        xn = xn_scr[...]
        half = wg_ref.shape[1] // 2
        out = None
        for c0 in (0, half):
            a = _dot(xn, wg_ref[:, c0:c0 + half])
            b = _dot(xn, wu_ref[:, c0:c0 + half])
            mid = (a * _sigmoid(a) * b).astype(BF16)
            part = _dot(mid, wd_ref[c0:c0 + half, :])
            out = part if out is None else out + part
        return out

    @pl.when(j == 0)
    def _():
        x_copy(i).wait()
        xn_scr[...] = _rms(xbuf[...], g_ref[...]).astype(BF16)
        o_ref[...] = 2.0 * xbuf[...] + partial_sum()

    @pl.when(j > 0)
    def _():
        o_ref[...] += partial_sum()

    @pl.when(jnp.logical_and(j == 1, i + 1 < n_i))
    def _():
        x_copy(i + 1).start()

    @pl.when(j == n_j - 1)
    def _():
        h = 0.5 * o_ref[...]
        o_ref[...] = _rms(h, g2_ref[...]) if final_norm else h


def _ffn(x, g, wg, wu, wd, g2=None):
    m, d = x.shape
    dff = wg.shape[1]
    tm = min(1024, m)
    tf = 512
    assert dff // tf >= 2
    row = pl.BlockSpec((tm, d), lambda i, j: (i, 0))
    vec = pl.BlockSpec((1, d), lambda i, j: (0, 0))
    args = [x, g, wg, wu, wd] + ([g2] if g2 is not None else [])
    return pl.pallas_call(
        functools.partial(_ffn_kernel, final_norm=g2 is not None),
        grid=(m // tm, dff // tf),
        in_specs=[pl.BlockSpec(memory_space=pl.ANY), vec,
                  pl.BlockSpec((d, tf), lambda i, j: (0, j)),
                  pl.BlockSpec((d, tf), lambda i, j: (0, j)),
                  pl.BlockSpec((tf, d), lambda i, j: (j, 0))] + ([vec] if g2 is not None else []),
        out_specs=row,
        out_shape=jax.ShapeDtypeStruct((m, d), F32),
        scratch_shapes=[pltpu.VMEM((tm, d), F32), pltpu.VMEM((tm, d), BF16),
                        pltpu.SemaphoreType.DMA(())],
        compiler_params=_cparams(("arbitrary", "arbitrary")),
        name="ffn",
    )(*args)


def _inproj_kernel(h_ref, g_ref, w_ref, o_ref, xn_scr):
    j = pl.program_id(1)
    tm = h_ref.shape[0]

    @pl.when(j == 0)
    def _():
        step = min(256, tm)
        for r0 in range(0, tm, step):
            rows = slice(r0, r0 + step)
            xn = _rms(h_ref[rows, :], g_ref[...]).astype(BF16)
            xn_scr[rows, :] = xn
            o_ref[rows, :] = _dot(xn, w_ref[...]).astype(o_ref.dtype)

    @pl.when(j > 0)
    def _():
        o_ref[...] = _dot(xn_scr[...], w_ref[...]).astype(o_ref.dtype)


def _inproj(h, g, w, out_dtype):
    m, k = h.shape
    n = w.shape[1]
    tm = min(1024, m)
    tn = 2048 if n % 2048 == 0 else 1024
    return pl.pallas_call(
        _inproj_kernel,
        grid=(m // tm, n // tn),
        in_specs=[pl.BlockSpec((tm, k), lambda i, j: (i, 0)),
                  pl.BlockSpec((1, k), lambda i, j: (0, 0)),
                  pl.BlockSpec((k, tn), lambda i, j: (0, j))],
        out_specs=pl.BlockSpec((tm, tn), lambda i, j: (i, j)),
        out_shape=jax.ShapeDtypeStruct((m, n), out_dtype),
        scratch_shapes=[pltpu.VMEM((tm, k), BF16)],
        compiler_params=_cparams(("parallel", "arbitrary")),
        name="inproj",
    )(h, g, w)


def _iota(shape, dim):
    return lax.broadcasted_iota(jnp.int32, shape, dim)


def _split(x, parts):
    out = []
    for _ in range(parts - 1):
        hi = x.astype(BF16)
        out.append(hi)
        x = x - hi.astype(F32)
    out.append(x.astype(BF16))
    return out


def _dot01(m01, x, parts):
    return sum(_dot(m01, xp) for xp in _split(x, parts))


def _wkv_masks(L):
    C, H = GROUP_COLS, GROUP_HEADS
    hl = H * L
    r_hl, c_c = np.arange(hl)[:, None], np.arange(C)[None, :]
    t, s = np.arange(L)[:, None], np.arange(hl)[None, :] % L
    assert hl == C
    seg = np.arange(C)[:, None] // RWKV_HEAD == c_c // RWKV_HEAD
    sq = np.stack([r_hl // L == c_c // RWKV_HEAD, r_hl // L == np.arange(hl)[None, :] // L, seg])
    inv = [t // INV_BASE == s // INV_BASE]
    m = INV_BASE
    while m < L:
        inv.append((t // (2 * m) == s // (2 * m)) & ((t // m) % 2 == 1) & ((s // m) % 2 == 0))
        m *= 2
    return (jnp.asarray(sq, BF16),
            jnp.asarray(seg, F32),
            jnp.asarray(np.stack([t > s, t >= s, t == s] + inv), F32),
            jnp.asarray(np.arange(L)[:, None] >= np.arange(L)[None, :], BF16))


def _mask_refs(sq_ref, cc_ref, tri3_ref, tri_ref):
    return dict(bd_lc=sq_ref.at[0], bd_ll=sq_ref.at[1], seg=sq_ref.at[2], bd_cc=cc_ref,
                strict=tri3_ref.at[0], incl=tri3_ref.at[1], eye=tri3_ref.at[2], tri=tri_ref,
                inv=[tri3_ref.at[i] for i in range(3, tri3_ref.shape[0])])


N_MASKS = 4
_VEC_ROWS = ("mu_r", "mu_k", "mu_v", "w0", "a0", "k_k", "k_a", "r_k", "gn_w", "gn_b")
_TAIL_NAMES = ("wl", "bonus", "gate")
_SLOT1_NAMES = ("ar", "bk", "vbd", "abd", "bkh", "v") + _TAIL_NAMES
_SLOT2_NAMES = ("q", "y0", "g", "n") + _TAIL_NAMES


def _slot_shapes(tt, L):
    C, n_c, hl = GROUP_COLS, tt // L, GROUP_HEADS * L
    return dict(ar=((n_c, 2 * L, C), BF16), bk=((n_c, 2 * hl, C), BF16), vbd=((n_c, hl, C), BF16),
                abd=((n_c, hl, C), BF16), bkh=((n_c, 2 * L, C), BF16), v=((tt, C), BF16),
                wl=((n_c, 1, C), F32), bonus=((tt, C), F32), gate=((tt, C), F32),
                q=((n_c, L, C), BF16), y0=((n_c, L, C), F32), g=((n_c, C, C), BF16),
                n=((n_c, C, C), F32))


def _shift_rows(x, carry_row):
    prev = pltpu.roll(x, 1, axis=0)
    return jnp.where(_iota(x.shape, 0) == 0, carry_row, prev)


def _bf(x):
    return x.astype(BF16)


def _cat(axis, *xs):
    return jnp.concatenate(xs, axis=axis)


def _tile_rows(x):
    return jnp.concatenate([x] * GROUP_HEADS, axis=0)


def _each(f, *lists):
    return [f(*xs) for xs in zip(*lists)]


def _lora_act_kernel(plo_ref, clo_ref, mulo_ref, act_ref):
    x = plo_ref[...].astype(F32)
    prev = _shift_rows(x, clo_ref[15:16, :].astype(F32))
    lo = x + mulo_ref[...] * (prev - x)
    act_ref[:, :LORA_PAD] = _bf(jnp.tanh(lo[:, :LORA_PAD]))
    act_ref[:, LORA_PAD:2 * LORA_PAD] = _bf(lo[:, LORA_PAD:2 * LORA_PAD])
    act_ref[:, 2 * LORA_PAD:] = _bf(_sigmoid(lo[:, 2 * LORA_PAD:]))


def _lora_act(proj, carry16, mu_lo, *, bsz, seq, off_lo):
    lo_w = 4 * LORA_PAD
    lb = off_lo // lo_w
    return pl.pallas_call(
        _lora_act_kernel,
        grid=(bsz,),
        in_specs=[pl.BlockSpec((seq, lo_w), lambda b: (b, lb)),
                  pl.BlockSpec((16, lo_w), lambda b: (0, lb)),
                  pl.BlockSpec((1, lo_w), lambda b: (0, 0))],
        out_specs=pl.BlockSpec((seq, lo_w), lambda b: (b, 0)),
        out_shape=jax.ShapeDtypeStruct((bsz * seq, lo_w), BF16),
        compiler_params=_cparams(("parallel",)),
        name="lora_act",
    )(proj, carry16, mu_lo)


def _rwkv_prep(tp, gp, L, mk, refs, wr):
    p_refs, c0_refs = refs[0:3], refs[3:6]
    act_ref, vec_ref, wlo_ref, carry_scr = refs[6:]
    tt = act_ref.shape[0]
    vec = lambda name: vec_ref[_VEC_ROWS.index(name):_VEC_ROWS.index(name) + 1, :]

    def shifted(j, mu):
        x = p_refs[j][...].astype(F32)
        carry = jnp.where(tp == 0, c0_refs[j][...], carry_scr[3 * gp + j])
        prev = _shift_rows(x, carry[15:16, :].astype(F32))
        carry_scr[3 * gp + j] = p_refs[j][tt - 16:tt, :]
        return x + mu * (prev - x)

    r = shifted(0, vec("mu_r"))
    k = shifted(1, vec("mu_k"))
    v = shifted(2, vec("mu_v"))
    kkr = k * vec("k_k")
    yield

    u = vec("w0") + _dot(act_ref[:, :LORA_PAD], wlo_ref[:LORA_PAD, :])
    lw = -(np.exp(-0.5) * np.log2(np.e)) * _sigmoid(u)
    alpha = _sigmoid(vec("a0") + _dot(act_ref[:, LORA_PAD:2 * LORA_PAD],
                                      wlo_ref[LORA_PAD:2 * LORA_PAD, :]))
    wr["gate"][...] = _dot(act_ref[:, 2 * LORA_PAD:], wlo_ref[2 * LORA_PAD:, :])
    kk = kkr * lax.rsqrt(jnp.maximum(_dot(_bf(kkr * kkr), mk["seg"][...]), 1e-24))
    k2 = k * (1.0 + (alpha - 1.0) * vec("k_a"))
    b = kk * alpha
    yield

    tri = mk["tri"][...]
    cum = _cat(0, *[_dot01(tri, lw[c * L:(c + 1) * L], 2) for c in range(tt // L)])
    wr["bonus"][...] = _dot(_bf(r * k2 * vec("r_k")), mk["seg"][...]) * v
    e_n = jnp.exp2(-cum)
    a_t = _bf(-kk * jnp.exp2(cum - lw))
    r_t = _bf(r * jnp.exp2(cum))
    b_t = _bf(b * e_n)
    k_t = _bf(k2 * e_n)
    vb = _bf(v)
    wr["v"][...] = vb
    bd_lc = mk["bd_lc"][...]
    for c in range(tt // L):
        sl = slice(c * L, (c + 1) * L)
        w_l = jnp.exp2(cum[sl.stop - 1:sl.stop, :])
        e_l = e_n[sl] * w_l
        wr["wl"][c] = w_l
        wr["ar"][c, :L] = a_t[sl]
        wr["ar"][c, L:] = r_t[sl]
        wr["bk"][c, :GROUP_HEADS * L] = _tile_rows(b_t[sl]) * bd_lc
        wr["bk"][c, GROUP_HEADS * L:] = _tile_rows(k_t[sl]) * bd_lc
        wr["vbd"][c] = _tile_rows(vb[sl]) * bd_lc
        wr["abd"][c] = _tile_rows(a_t[sl]) * bd_lc
        wr["bkh"][c, :L] = _bf(b[sl] * e_l)
        wr["bkh"][c, L:] = _bf(k2[sl] * e_l)


def _rwkv_local(L, mk, rd, wr):
    tt = rd["v"].shape[0]
    n_c = tt // L
    n_h = GROUP_HEADS * L
    C = GROUP_COLS
    chunks = [slice(c * L, (c + 1) * L) for c in range(n_c)]
    bd_lc, bd_ll = mk["bd_lc"][...], mk["bd_ll"][...]
    strict, incl = mk["strict"][...] > 0, mk["incl"][...] > 0
    bd_cc = mk["bd_cc"][...]
    bd = lambda xb: _tile_rows(xb) * bd_lc
    bd_sq = lambda pb: _tile_rows(pb) * bd_ll
    for k in _TAIL_NAMES:
        wr[k][...] = rd[k][...]

    ar = [rd["ar"][c] for c in range(n_c)]
    aa = _each(_dot_nt, ar, [rd["bk"][c] for c in range(n_c)])
    yield
    a_ab = _each(lambda x: jnp.where(strict, x[:L, :n_h], 0.0), aa)
    a_ak = _each(lambda x: _bf(jnp.where(strict, x[:L, n_h:], 0.0)), aa)
    a_rb = _each(lambda x: _bf(jnp.where(incl, x[L:, :n_h], 0.0)), aa)
    a_rk = _each(lambda x: _bf(jnp.where(incl, x[L:, n_h:], 0.0)), aa)

    eye = mk["eye"][...]
    d_base = mk["inv"][0][...]
    p = _each(lambda x: x * d_base, a_ab)
    tp = _each(lambda x: eye + x, p)
    for i in range(int(np.log2(INV_BASE)) - 1):
        if i == 0:
            p = _each(lambda x: _dot(_bf(x), bd_sq(_bf(x))), p)
        else:
            out = _each(lambda x, t: _dot(_bf(_cat(0, x, t)), bd_sq(_bf(x))), p, tp)
            p = _each(lambda o: o[:L], out)
            tp = _each(lambda t, o: t + o[L:], tp, out)
        yield
    tp = _each(lambda t, x: t + _dot(_bf(t), bd_sq(_bf(x))), tp, p)
    yield
    for off_ref in mk["inv"][1:]:
        off = off_ref[...]
        y = _each(lambda x, t: _dot(_bf(x * off), bd_sq(_bf(t))), a_ab, tp)
        yield
        tp = _each(lambda t, x: t + _dot(_bf(t), bd_sq(_bf(x))), tp, y)
        yield
    tb = _each(_bf, tp)

    v_bd = [rd["vbd"][c] for c in range(n_c)]
    x0 = _each(_dot, a_ak, v_bd)
    yield
    ut = _each(lambda t, x, c: _dot(t, _cat(1, bd(_bf(x)), rd["abd"][c])), tb, x0, range(n_c))
    yield
    u0 = _each(lambda x: _bf(x[:, :C]), ut)
    ta = _each(lambda x: _bf(x[:, C:]), ut)
    for c in range(n_c):
        wr["q"][c] = _bf(ar[c][L:].astype(F32) + _dot(a_rb[c], bd(ta[c])))
    yield
    for c in range(n_c):
        wr["y0"][c] = _dot(_cat(1, a_rb[c], a_rk[c]), _cat(0, bd(u0[c]), v_bd[c]))
    yield
    for c in range(n_c):
        bkh = rd["bkh"][c]
        wr["g"][c] = _bf(_dot_tn(ta[c], bkh[:L]) * bd_cc)
        wr["n"][c] = _dot_tn(_cat(0, u0[c], rd["v"][chunks[c]]), bkh) * bd_cc


def _rwkv_serial(tc, gc, L, mk, refs, rd, emit_state):
    s0_ref, vec_ref, y_ref, sout_ref, s_scr = refs
    gnw = vec_ref[_VEC_ROWS.index("gn_w"):_VEC_ROWS.index("gn_w") + 1, :]
    gnb = vec_ref[_VEC_ROWS.index("gn_b"):_VEC_ROWS.index("gn_b") + 1, :]
    n_c = rd["q"].shape[0]
    S = jnp.where(tc == 0, s0_ref[gc], s_scr[gc])
    ys = []
    for c in range(n_c):
        sb = _bf(S)
        ys.append(rd["y0"][c] + _dot_nt(rd["q"][c], sb))
        S = S * rd["wl"][c] + _dot(sb, rd["g"][c]) + rd["n"][c]
        yield
    s_scr[gc] = S
    if emit_state:
        sout_ref[0] = S
    y = ys[0] if n_c == 1 else _cat(0, *ys)

    inv_n = 1.0 / RWKV_HEAD
    seg = mk["seg"][...]
    mean = _dot(_bf(y), seg) * inv_n
    yield
    d = y - mean
    var = _dot(_bf(d * d), seg) * inv_n
    yield
    yn = d * lax.rsqrt(var + GN_EPS) * gnw + gnb
    y_ref[...] = ((yn + rd["bonus"][...]) * rd["gate"][...]).astype(y_ref.dtype)


def _weave(gens, order):
    for who in order:
        next(gens[who], None)
    for gen in gens.values():
        for _ in gen:
            pass


_WEAVE = "ppp" + "ls" * 12


def _rwkv_kernel(*refs, L, n_t, n_g, emit_state):
    n_prep = 9
    n_in = n_prep + N_MASKS + 2
    in_refs, rest = refs[:n_in], refs[n_in:]
    prep_in, chain_in = in_refs[:n_prep], in_refs[-2:]
    mk = _mask_refs(*in_refs[n_prep:n_prep + N_MASKS])
    if emit_state:
        y_ref, sout_ref = rest[:2]
        scr = rest[2:]
    else:
        y_ref, sout_ref = rest[0], None
        scr = rest[1:]
    s_scr, carry_scr = scr[:2]
    scr = list(scr[2:])
    slots1 = [dict(zip(_SLOT1_NAMES, [scr.pop(0) for _ in _SLOT1_NAMES])) for _ in range(2)]
    slots2 = [dict(zip(_SLOT2_NAMES, [scr.pop(0) for _ in _SLOT2_NAMES])) for _ in range(2)]

    s = pl.program_id(0)
    n = pl.num_programs(0) - 2
    sp = jnp.minimum(s, n - 1)
    sc = jnp.clip(s - 2, 0, n - 1)
    tp, gp = (sp // n_g) % n_t, sp % n_g
    tc, gc = (sc // n_g) % n_t, sc % n_g

    @pl.when(s == 0)
    def _():
        for ref in (tuple(slots1[1].values()) + tuple(slots2[0].values()) + tuple(slots2[1].values())
                    + (s_scr, carry_scr)):
            ref[...] = jnp.zeros_like(ref)

    def step(parity):
        gens = dict(
            p=_rwkv_prep(tp, gp, L, mk, prep_in + (carry_scr,), slots1[parity]),
            l=_rwkv_local(L, mk, slots1[1 - parity], slots2[1 - parity]),
            s=_rwkv_serial(tc, gc, L, mk, chain_in + (y_ref, sout_ref, s_scr), slots2[parity],
                           emit_state))
        _weave(gens, _WEAVE)

    for parity in range(2):
        pl.when(s % 2 == parity)(functools.partial(step, parity))


def _rwkv(proj, act, carry16, s0, rp, *, bsz, seq, tt, d, off_rkv, emit_state):
    C = GROUP_COLS
    n_g = d // C
    L = min(CHUNK, tt)
    n_t = seq // tt
    assert n_t == 1 or not emit_state
    n = bsz * n_t * n_g
    lo_w = 4 * LORA_PAD
    cb = off_rkv // C
    masks = _wkv_masks(L)

    def tile(s):
        return s // n_g, s % n_g

    def prep_ix(f):
        return lambda s: f(*tile(jnp.minimum(s, n - 1)))

    def serial_ix(f):
        return lambda s: f(*tile(jnp.clip(s - 2, 0, n - 1)))

    const = lambda a: pl.BlockSpec(a.shape, lambda s, nd=a.ndim: (0,) * nd)
    n_vec = len(_VEC_ROWS)
    rkv = lambda rows: [pl.BlockSpec((rows, C), prep_ix(lambda r, g, j=j: (r if rows == tt else 0,
                                                                        cb + j * n_g + g)))
                        for j in range(3)]
    in_specs = rkv(tt) + rkv(16) + [
        pl.BlockSpec((tt, lo_w), prep_ix(lambda r, g: (r, 0))),
        pl.BlockSpec((n_vec, C), prep_ix(lambda r, g: (0, g))),
        pl.BlockSpec((lo_w, C), prep_ix(lambda r, g: (0, g))),
    ] + [const(m) for m in masks] + [
        const(s0),
        pl.BlockSpec((n_vec, C), serial_ix(lambda r, g: (0, g))),
    ]
    out_specs = [pl.BlockSpec((tt, C), serial_ix(lambda r, g: (r, g)))]
    out_shape = [jax.ShapeDtypeStruct((bsz * seq, d), BF16)]
    if emit_state:
        out_specs.append(pl.BlockSpec((1, C, C), serial_ix(lambda r, g: (r * n_g + g, 0, 0))))
        out_shape.append(jax.ShapeDtypeStruct((bsz * n_g, C, C), F32))
    shp = _slot_shapes(tt, L)
    scratch = [pltpu.VMEM((n_g, C, C), F32), pltpu.VMEM((3 * n_g, 16, C), BF16)]
    scratch += [pltpu.VMEM(*shp[k]) for _ in range(2) for k in _SLOT1_NAMES]
    scratch += [pltpu.VMEM(*shp[k]) for _ in range(2) for k in _SLOT2_NAMES]
    return pl.pallas_call(
        functools.partial(_rwkv_kernel, L=L, n_t=n_t, n_g=n_g, emit_state=emit_state),
        grid=(n + 2,),
        in_specs=in_specs,
        out_specs=out_specs,
        out_shape=out_shape,
        scratch_shapes=scratch,
        compiler_params=_cparams(("arbitrary",)),
        name="rwkv",
    )(proj, proj, proj, carry16, carry16, carry16, act, rp["vec"], rp["w_lora"], *masks, s0,
      rp["vec"])


def _rope(x, c, s1, s2):
    return x * c + pltpu.roll(x, 96, axis=1) * s1 + pltpu.roll(x, 32, axis=1) * s2


def _mla_prep_kernel(cq_ref, ckv_ref, kpe_ref, qn_ref, kvn_ref, wq_ref, wkv_ref,
                     c_ref, s1_ref, s2_ref, q_ref, kn_ref, v_ref, kr_ref, *, n_heads):
    c, s1, s2 = c_ref[...], s1_ref[...], s2_ref[...]
    scale = QK_DIM ** -0.5 * np.log2(np.e)
    cq = _rms(cq_ref[...].astype(F32), qn_ref[...]).astype(BF16)
    for h in range(n_heads):
        qh = _dot(cq, wq_ref[:, h * QHEAD_COLS:(h + 1) * QHEAD_COLS]) * scale
        q_ref[:, h * QHEAD_COLS:h * QHEAD_COLS + LANES] = qh[:, :LANES].astype(BF16)
        q_ref[:, h * QHEAD_COLS + LANES:(h + 1) * QHEAD_COLS] = (
            _rope(qh[:, LANES:], c, s1, s2).astype(BF16))
    ckv = _rms(ckv_ref[...].astype(F32), kvn_ref[...]).astype(BF16)
    hw = n_heads * NOPE_DIM
    kn_ref[...] = _dot(ckv, wkv_ref[:, :hw]).astype(BF16)
    v_ref[...] = _dot(ckv, wkv_ref[:, hw:]).astype(BF16)
    kr_ref[...] = _rope(kpe_ref[...].astype(F32), c, s1, s2).astype(BF16)


def _mla_prep(proj, mp, tabs, *, seq, n_heads, off_cq, off_ckv, off_kpe):
    m = proj.shape[0]
    tm = min(512, seq)
    n_t = seq // tm
    ql = mp["w_uq"].shape[0]
    kl = mp["w_ukv"].shape[0]
    hw = n_heads * NOPE_DIM
    tab = pl.BlockSpec((tm, LANES), lambda i: (i % n_t, 0))
    full = lambda a: pl.BlockSpec(a.shape, lambda i: (0, 0))
    return pl.pallas_call(
        functools.partial(_mla_prep_kernel, n_heads=n_heads),
        grid=(m // tm,),
        in_specs=[pl.BlockSpec((tm, ql), lambda i: (i, off_cq // ql)),
                  pl.BlockSpec((tm, kl), lambda i: (i, off_ckv // kl)),
                  pl.BlockSpec((tm, LANES), lambda i: (i, off_kpe // LANES)),
                  full(mp["q_norm"]), full(mp["kv_norm"]), full(mp["w_uq"]), full(mp["w_ukv"]),
                  tab, tab, tab],
        out_specs=[pl.BlockSpec((tm, n_heads * QHEAD_COLS), lambda i: (i, 0)),
                   pl.BlockSpec((tm, hw), lambda i: (i, 0)),
                   pl.BlockSpec((tm, hw), lambda i: (i, 0)),
                   pl.BlockSpec((tm, LANES), lambda i: (i, 0))],
        out_shape=[jax.ShapeDtypeStruct((m, n_heads * QHEAD_COLS), BF16),
                   jax.ShapeDtypeStruct((m, hw), BF16),
                   jax.ShapeDtypeStruct((m, hw), BF16),
                   jax.ShapeDtypeStruct((m, LANES), BF16)],
        compiler_params=_cparams(("parallel",)),
        name="mla_prep",
    )(proj, proj, proj, mp["q_norm"], mp["kv_norm"], mp["w_uq"], mp["w_ukv"], *tabs)


def _attn_kernel(q_ref, kn_ref, kr_ref, v_ref, km_ref, vmt_ref, o_ref, *, tq):
    seq = q_ref.shape[0]
    n_q = seq // tq
    mp = km_ref.shape[1]
    kx = jnp.concatenate([km_ref[0], jnp.concatenate([kn_ref[...], kr_ref[...]], axis=1)], axis=0)
    vxt = jnp.concatenate([vmt_ref[0], v_ref[...].astype(F32).T.astype(BF16)], axis=1)
    qs = [q_ref[i * tq:(i + 1) * tq, :] for i in range(n_q)]
    n_k = [mp + tq * (i + 1) for i in range(n_q)]
    meta_ok = _iota((mp, tq), 0) < N_META
    causal = _iota((tq, tq), 0) <= _iota((tq, tq), 1)

    ss = [_dot_nt(kx[:n], q) for n, q in zip(n_k, qs)]
    blocks = [[jnp.where(meta_ok, s[:mp], -1e30)] + ([s[mp:n - tq]] if n - tq > mp else [])
              + [jnp.where(causal, s[n - tq:], -1e30)] for n, s in zip(n_k, ss)]
    col_max = lambda x: jnp.max(x, axis=0, keepdims=True)
    ms = [functools.reduce(jnp.maximum, [col_max(x) for x in bl]) for bl in blocks]
    ps = [[jnp.exp2(x - m) for x in bl] for bl, m in zip(blocks, ms)]
    ls = [sum(jnp.sum(x, axis=0, keepdims=True) for x in pl_) for pl_ in ps]
    pv = [_dot(vxt[:, :n], jnp.concatenate([x.astype(BF16) for x in pl_], axis=0))
          for n, pl_ in zip(n_k, ps)]
    for i in range(n_q):
        o_ref[i * tq:(i + 1) * tq, :] = (pv[i] / ls[i]).T.astype(o_ref.dtype)


def _attn(q, kn, kr, v, k_meta, vt_meta, *, bsz, seq, n_heads):
    tq = min(ATTN_TILE, seq)
    mp = k_meta.shape[1]
    return pl.pallas_call(
        functools.partial(_attn_kernel, tq=tq),
        grid=(bsz, n_heads),
        in_specs=[pl.BlockSpec((seq, QHEAD_COLS), lambda b, h: (b, h)),
                  pl.BlockSpec((seq, NOPE_DIM), lambda b, h: (b, h)),
                  pl.BlockSpec((seq, LANES), lambda b, h: (b, 0)),
                  pl.BlockSpec((seq, V_DIM), lambda b, h: (b, h)),
                  pl.BlockSpec((1, mp, QHEAD_COLS), lambda b, h: (h, 0, 0)),
                  pl.BlockSpec((1, V_DIM, mp), lambda b, h: (h, 0, 0))],
        out_specs=pl.BlockSpec((seq, V_DIM), lambda b, h: (b, h)),
        out_shape=jax.ShapeDtypeStruct((bsz * seq, n_heads * V_DIM), BF16),
        compiler_params=_cparams(("parallel", "parallel")),
        name="attn",
    )(q, kn, kr, v, k_meta, vt_meta)


def _outproj_kernel(h_ref, ya_ref, yb_ref, ga_ref, gb_ref, w_ref, o_ref):
    mix = (_sigmoid(ga_ref[...].astype(F32)) * ya_ref[...].astype(F32)
           + _sigmoid(gb_ref[...].astype(F32)) * yb_ref[...].astype(F32))
    o_ref[...] = h_ref[...] + _dot(mix.astype(BF16), w_ref[...])


def _outproj(h, ya, yb, proj, w_out, *, off_gate):
    m, d = h.shape
    tm = min(512, m)
    gb = off_gate // d
    row = pl.BlockSpec((tm, d), lambda i: (i, 0))
    return pl.pallas_call(
        _outproj_kernel,
        grid=(m // tm,),
        in_specs=[row, row, row,
                  pl.BlockSpec((tm, d), lambda i: (i, gb)),
                  pl.BlockSpec((tm, d), lambda i: (i, gb + 1)),
                  pl.BlockSpec((d, d), lambda i: (0, 0))],
        out_specs=row,
        out_shape=jax.ShapeDtypeStruct((m, d), F32),
        compiler_params=_cparams(("parallel",)),
        name="outproj",
    )(h, ya, yb, proj, proj, w_out)


def _pad_cols(w, n):
    return jnp.pad(w, ((0, 0), (0, n - w.shape[1])))


def _pad_rows(w, n):
    return jnp.pad(w, ((0, n - w.shape[0]), (0, 0)))


def _rope_tables(pos):
    inv_freq = (1.0 / (np.float32(ROPE_THETA) ** (np.arange(0, ROPE_DIM, 2, dtype=np.float32)
                                                  / np.float32(ROPE_DIM)))).astype(np.float32)
    ang = pos.astype(np.float32)[:, None] * inv_freq[None, :]
    cos, sin = np.cos(ang), np.sin(ang)
    z = np.zeros_like(cos)
    return tuple(jnp.asarray(np.concatenate(t, axis=1), F32)
                 for t in ([cos, cos, z, z], [-sin, z, z, z], [z, sin, z, z]))


def kernel(x, meta_tokens, ffn1_norm, ffn1_w_gate, ffn1_w_up, ffn1_w_down, mix_norm, w_in,
           tm_mu, w0, w_up, a0, a_up, g_up, k_k, k_a, r_k, gn_w, gn_b, q_norm, w_uq,
           kv_norm, w_ukv, w_out, ffn2_norm, ffn2_w_gate, ffn2_w_up, ffn2_w_down, final_norm):
    bsz, seq, d = x.shape
    assert d % GROUP_COLS == 0 and seq % 16 == 0
    assert ffn1_norm.shape[0] == 1, "single-layer stack"
    n_mla = d // 128
    w_lora, a_lora, g_lora = w_up.shape[1], a_up.shape[1], g_up.shape[1]
    q_lora, kv_lora = w_uq.shape[1], w_ukv.shape[1]
    assert w_lora <= LORA_PAD and a_lora <= LORA_PAD and g_lora == 2 * LORA_PAD
    rwkv_cols = 3 * d + w_lora + a_lora + g_lora
    mla_cols = q_lora + kv_lora + ROPE_DIM

    wi = w_in[0]
    o1, o2 = 3 * d, 3 * d + w_lora
    o3 = o2 + a_lora
    wi_rkv = wi[:, :3 * d]
    wi_lo = jnp.concatenate([_pad_cols(wi[:, o1:o2], LORA_PAD), _pad_cols(wi[:, o2:o3], LORA_PAD),
                             wi[:, o3:rwkv_cols]], axis=1)
    m0 = rwkv_cols
    wi_cq = wi[:, m0:m0 + q_lora]
    wi_ckv = wi[:, m0 + q_lora:m0 + q_lora + kv_lora]
    wi_kpe = _pad_cols(wi[:, m0 + q_lora + kv_lora:m0 + mla_cols], LANES)
    wi_gate = wi[:, m0 + mla_cols:]
    parts = [wi_rkv, wi_gate, wi_lo, wi_cq, wi_ckv, wi_kpe]
    offs = np.cumsum([0] + [p.shape[1] for p in parts])
    off_rkv, off_gate, off_lo, off_cq, off_ckv, off_kpe, n_used = (int(o) for o in offs)
    n_proj = -(-n_used // 1024) * 1024
    parts.append(jnp.zeros((d, n_proj - n_used), wi.dtype))
    w_proj = jnp.concatenate([p.astype(BF16) for p in parts], axis=1)

    mu = tm_mu[0]
    vec = dict(mu_r=mu[None, :d], mu_k=mu[None, d:2 * d], mu_v=mu[None, 2 * d:3 * d], w0=w0, a0=a0,
               k_k=k_k, k_a=k_a, r_k=r_k[0].reshape(1, d), gn_w=gn_w, gn_b=gn_b)
    rp = dict(
        mu_lo=jnp.concatenate([jnp.pad(mu[o1:o2], (0, LORA_PAD - w_lora)),
                               jnp.pad(mu[o2:o3], (0, LORA_PAD - a_lora)), mu[o3:]])[None, :],
        vec=jnp.concatenate([vec[name] for name in _VEC_ROWS], axis=0),
        w_lora=jnp.concatenate([_pad_rows(w_up[0], LORA_PAD), _pad_rows(a_up[0], LORA_PAD),
                                g_up[0]], axis=0).astype(BF16),
    )

    wq = w_uq[0].reshape(q_lora, n_mla, QK_DIM)
    wq = jnp.pad(wq, ((0, 0), (0, 0), (0, QHEAD_COLS - QK_DIM))).reshape(q_lora, n_mla * QHEAD_COLS)
    wkv = w_ukv[0].reshape(kv_lora, n_mla, NOPE_DIM + V_DIM)
    wkv = jnp.concatenate([wkv[:, :, :NOPE_DIM].reshape(kv_lora, -1),
                           wkv[:, :, NOPE_DIM:].reshape(kv_lora, -1)], axis=1)
    mp = dict(q_norm=q_norm, kv_norm=kv_norm, w_uq=wq.astype(BF16), w_ukv=wkv.astype(BF16))

    f1 = (_to_bf16(ffn1_w_gate[0]), _to_bf16(ffn1_w_up[0]), _to_bf16(ffn1_w_down[0]))
    f2 = (_to_bf16(ffn2_w_gate[0]), _to_bf16(ffn2_w_up[0]), _to_bf16(ffn2_w_down[0]))
    w_o = w_out[0].astype(BF16)

    def front(h_rows, *, b, s, pos, carry16, s0, emit_state):
        h1 = _ffn(h_rows, ffn1_norm, *f1)
        proj = _inproj(h1, mix_norm, w_proj, BF16)
        tt = min(RWKV_TILE, s)
        act = _lora_act(proj, carry16, rp["mu_lo"], bsz=b, seq=s, off_lo=off_lo)
        rw = _rwkv(proj, act, carry16, s0, rp, bsz=b, seq=s, tt=tt, d=d, off_rkv=off_rkv,
                   emit_state=emit_state)
        q, kn, v, kr = _mla_prep(proj, mp, _rope_tables(pos), seq=s, n_heads=n_mla,
                                 off_cq=off_cq, off_ckv=off_ckv, off_kpe=off_kpe)
        return h1, proj, rw, (q, kn, kr, v)

    n_g = d // GROUP_COLS
    pad = META_ROWS - N_META
    meta_h = jnp.concatenate([jnp.zeros((pad, d), F32), meta_tokens.astype(F32)], axis=0)
    meta_pos = np.maximum(np.arange(META_ROWS) - pad, 0)
    _, proj_m, (_, s_meta), (_, kn_m, kr_m, v_m) = front(
        meta_h, b=1, s=META_ROWS, pos=meta_pos,
        carry16=jnp.zeros((16, n_proj), BF16), s0=jnp.zeros((n_g, GROUP_COLS, GROUP_COLS), F32),
        emit_state=True)
    kn_m = kn_m[pad:].reshape(N_META, n_mla, NOPE_DIM).transpose(1, 0, 2)
    kr_m = jnp.broadcast_to(kr_m[pad:][None], (n_mla, N_META, LANES))
    k_meta = jnp.pad(jnp.concatenate([kn_m, kr_m], axis=2), ((0, 0), (0, LANES - N_META), (0, 0)))
    vt_meta = jnp.pad(v_m[pad:].reshape(N_META, n_mla, V_DIM).transpose(1, 2, 0),
                      ((0, 0), (0, 0), (0, LANES - N_META)))

    h0 = x.reshape(bsz * seq, d)
    h1, proj, (ya,), (q, kn, kr, v) = front(
        h0, b=bsz, s=seq, pos=N_META + np.arange(seq),
        carry16=proj_m[META_ROWS - 16:], s0=s_meta, emit_state=False)
    yb = _attn(q, kn, kr, v, k_meta, vt_meta, bsz=bsz, seq=seq, n_heads=n_mla)
    h2 = _outproj(h1, ya, yb, proj, w_o, off_gate=off_gate)
    y = _ffn(h2, ffn2_norm, *f2, final_norm[None, :])
    return y.reshape(bsz, seq, d)
```
